```python
import jax, jax.numpy as jnp
from jax import lax
import numpy as np

D_MODEL = 1024
BATCH = 8
SEQ = 2048
DEPTH = 1

HEAD_DIM = 64
N_HEADS_SB = 8
N_HEADS_SA = 8
N_KV_SA = 2
N_IDX_HEADS = 8
IDX_DIM = 64
TOPK_MAX = 256
ROPE_THETA = 500000.0
ROT_DIM = HEAD_DIM // 4
Q_BLOCK = 128
W_SB = N_HEADS_SB * HEAD_DIM
W_SA = N_HEADS_SA * HEAD_DIM
W_KV = N_KV_SA * HEAD_DIM
SPLIT_SIZES = (W_SB, W_SB, W_SB,
               W_SA, W_KV, W_KV,
               N_IDX_HEADS * IDX_DIM, IDX_DIM,
               N_IDX_HEADS,
               D_MODEL, D_MODEL)
IN_WIDTH = W_SB * 3 + W_SA + 2 * W_KV + N_IDX_HEADS * IDX_DIM + IDX_DIM + N_IDX_HEADS + 2 * D_MODEL

PEER_HEADS = 8
PEER_NKEYS = 128
PEER_EXPERTS = PEER_NKEYS * PEER_NKEYS
PEER_QDIM = 256
PEER_TOPK = 16
PEER_CHUNK = 128

NORM_EPS = 1e-6

kernel_name = "hybrid_stickbreak_dsa_peer_block"


def rmsnorm(x, g):
    xf = x.astype(jnp.float32)
    y = xf * lax.rsqrt(jnp.mean(xf * xf, axis=-1, keepdims=True) + NORM_EPS)
    return (y * g.astype(jnp.float32)).astype(x.dtype)


def partial_rope(x, positions):
    half = ROT_DIM // 2
    inv_freq = jnp.power(ROPE_THETA, -2.0 * jnp.arange(half, dtype=jnp.float32) / ROT_DIM)
    ang = positions.astype(jnp.float32)[..., None] * inv_freq
    cos = jnp.cos(ang)[:, :, None, :]
    sin = jnp.sin(ang)[:, :, None, :]
    xf = x.astype(jnp.float32)
    x1 = xf[..., :half]
    x2 = xf[..., half:ROT_DIM]
    out = jnp.concatenate([x1 * cos - x2 * sin, x2 * cos + x1 * sin, xf[..., ROT_DIM:]], axis=-1)
    return out.astype(x.dtype)


def stick_breaking_attention(q, k, v):
    B, T, H, d = q.shape
    scale = d ** -0.5
    kpos = jnp.arange(T)

    def block(i):
        t0 = i * Q_BLOCK
        qb = lax.dynamic_slice_in_dim(q, t0, Q_BLOCK, axis=1)
        z = jnp.einsum('bqhd,bshd->bhqs', qb, k).astype(jnp.float32) * scale
        qpos = t0 + jnp.arange(Q_BLOCK)
        strict = (kpos[None, :] < qpos[:, None])[None, None]
        log_fail = jnp.where(strict, jax.nn.log_sigmoid(-z), 0.0)
        between = lax.cumsum(log_fail, axis=3, reverse=True) - log_fail
        a = jnp.where(strict, jnp.exp(jax.nn.log_sigmoid(z) + between), 0.0)
        return jnp.einsum('bhqs,bshd->bqhd', a.astype(v.dtype), v)

    out = lax.map(block, jnp.arange(T // Q_BLOCK))
    return out.transpose(1, 0, 2, 3, 4).reshape(B, T, H, d)


def dsa_attention(q, k, v, q_idx, k_idx, w_idx):
    B, T, H, d = q.shape
    hkv = k.shape[2]
    groups = H // hkv
    k_sel = min(TOPK_MAX, T // 4)
    scale = d ** -0.5
    kpos = jnp.arange(T)
    gather = jax.vmap(lambda table, ids: table[ids])

    def block(i):
        t0 = i * Q_BLOCK
        qpos = t0 + jnp.arange(Q_BLOCK)
        qi = lax.dynamic_slice_in_dim(q_idx, t0, Q_BLOCK, axis=1)
        wi = lax.dynamic_slice_in_dim(w_idx, t0, Q_BLOCK, axis=1).astype(jnp.float32)
        rel = jax.nn.relu(jnp.einsum('bqhd,bsd->bqhs', qi, k_idx).astype(jnp.float32) * IDX_DIM ** -0.5)
        score = jnp.einsum('bqhs,bqh->bqs', rel, wi) * N_IDX_HEADS ** -0.5
        causal = (kpos[None, :] <= qpos[:, None])[None]
        score = jnp.where(causal, score, -jnp.inf)
        _, idx = lax.top_k(score, k_sel)
        valid = idx <= qpos[None, :, None]
        kg = gather(k, idx)
        vg = gather(v, idx)
        qb = lax.dynamic_slice_in_dim(q, t0, Q_BLOCK, axis=1).reshape(B, Q_BLOCK, hkv, groups, d)
        logits = jnp.einsum('bqhgd,bqkhd->bqhgk', qb, kg).astype(jnp.float32) * scale
        logits = jnp.where(valid[:, :, None, None, :], logits, -jnp.inf)
        p = jax.nn.softmax(logits, axis=-1)
        o = jnp.einsum('bqhgk,bqkhd->bqhgd', p.astype(v.dtype), vg)
        return o.reshape(B, Q_BLOCK, H * d)

    out = lax.map(block, jnp.arange(T // Q_BLOCK))
    return out.transpose(1, 0, 2, 3).reshape(B, T, H * d)


def token_mixer(h, positions, w_in, w_up_sb, w_up_sa, w_out):
    B, T, _ = h.shape
    proj = h @ w_in
    cuts = [int(s) for s in np.cumsum(SPLIT_SIZES)[:-1]]
    (q_sb, k_sb, v_sb, q_sa, k_sa, v_sa, q_ix, k_ix, w_ix, gate_sb, gate_sa) = jnp.split(proj, cuts, axis=-1)
    heads = lambda t, n: t.reshape(B, T, n, HEAD_DIM)
    o_sb = stick_breaking_attention(heads(q_sb, N_HEADS_SB), heads(k_sb, N_HEADS_SB), heads(v_sb, N_HEADS_SB))
    q_sa = partial_rope(heads(q_sa, N_HEADS_SA), positions)
    k_sa = partial_rope(heads(k_sa, N_KV_SA), positions)
    q_ix = partial_rope(q_ix.reshape(B, T, N_IDX_HEADS, IDX_DIM), positions)
    k_ix = partial_rope(k_ix[:, :, None, :], positions)[:, :, 0, :]
    o_sa = dsa_attention(q_sa, k_sa, heads(v_sa, N_KV_SA), q_ix, k_ix, w_ix)
    y_sb = o_sb.reshape(B, T, W_SB) @ w_up_sb
    y_sa = o_sa @ w_up_sa
    merged = jax.nn.sigmoid(gate_sb) * y_sb + jax.nn.sigmoid(gate_sa) * y_sa
    return merged @ w_out


def peer_ffn(h, w_query, sub_keys, expert_in, expert_out):
    B, T, D = h.shape
    n_tok = B * T
    xt = h.reshape(n_tok, D)
    q = (xt @ w_query).reshape(n_tok, PEER_HEADS, 2, PEER_QDIM // 2)
    s = jnp.einsum('nhpd,hpkd->nhpk', q, sub_keys).astype(jnp.float32)
    sv, si = lax.top_k(s, PEER_TOPK)
    cand = (sv[:, :, 0, :, None] + sv[:, :, 1, None, :]).reshape(n_tok, PEER_HEADS, PEER_TOPK * PEER_TOPK)
    cid = (si[:, :, 0, :, None] * PEER_NKEYS + si[:, :, 1, None, :]).reshape(n_tok, PEER_HEADS, PEER_TOPK * PEER_TOPK)
    top_s, top_p = lax.top_k(cand, PEER_TOPK)
    eid = jnp.take_along_axis(cid, top_p, axis=-1)
    g = jax.nn.softmax(top_s, axis=-1)
    n_chunks = n_tok // PEER_CHUNK

    def chunk(args):
        xc, ec, gc = args
        a = jnp.einsum('cd,chkd->chk', xc, expert_in[ec])
        act = gc.astype(xc.dtype) * jax.nn.gelu(a, approximate=False)
        return jnp.einsum('chk,chkd->cd', act, expert_out[ec])

    out = lax.map(chunk, (xt.reshape(n_chunks, PEER_CHUNK, D),
                          eid.reshape(n_chunks, PEER_CHUNK, PEER_HEADS, PEER_TOPK),
                          g.reshape(n_chunks, PEER_CHUNK, PEER_HEADS, PEER_TOPK)))
    return out.reshape(B, T, D)


def setup_inputs(seed: int = 0) -> dict:
    key = jax.random.key(seed)
    ks = jax.random.split(key, 20)
    f32 = jnp.float32
    nrm = lambda k, shape, s: jax.random.normal(k, shape, f32) * s
    D = D_MODEL
    x = jax.random.normal(ks[0], (BATCH, SEQ, D), f32)
    c = jax.random.normal(ks[1], (BATCH, D), f32)
    offsets = jax.random.randint(ks[2], (BATCH, 1), 0, 1024, dtype=jnp.int32)
    positions = jnp.arange(SEQ, dtype=jnp.int32)[None, :] + offsets
    return {
        "x": x,
        "c": c,
        "positions": positions,
        "w_ada": nrm(ks[3], (DEPTH, D, 6 * D), 0.5 * D ** -0.5),
        "b_ada": nrm(ks[4], (DEPTH, 6 * D), 0.02),
        "g_pre_mix": 1.0 + nrm(ks[5], (DEPTH, D), 0.02),
        "g_post_mix": 1.0 + nrm(ks[6], (DEPTH, D), 0.02),
        "w_in": nrm(ks[7], (DEPTH, D, IN_WIDTH), D ** -0.5),
        "w_up_sb": nrm(ks[8], (DEPTH, W_SB, D), W_SB ** -0.5),
        "w_up_sa": nrm(ks[9], (DEPTH, W_SA, D), W_SA ** -0.5),
        "w_out": nrm(ks[10], (DEPTH, D, D), D ** -0.5),
        "g_pre_ffn": 1.0 + nrm(ks[11], (DEPTH, D), 0.02),
        "g_post_ffn": 1.0 + nrm(ks[12], (DEPTH, D), 0.02),
        "w_peer_query": nrm(ks[13], (DEPTH, D, PEER_HEADS * PEER_QDIM), D ** -0.5),
        "peer_sub_keys": nrm(ks[14], (DEPTH, PEER_HEADS, 2, PEER_NKEYS, PEER_QDIM // 2), (PEER_QDIM // 2) ** -0.5),
        "peer_expert_in": nrm(ks[15], (DEPTH, PEER_EXPERTS, D), D ** -0.5),
        "peer_expert_out": nrm(ks[16], (DEPTH, PEER_EXPERTS, D), D ** -0.5),
    }


def reference(x, c, positions, w_ada, b_ada, g_pre_mix, g_post_mix, w_in, w_up_sb, w_up_sa, w_out,
              g_pre_ffn, g_post_ffn, w_peer_query, peer_sub_keys, peer_expert_in, peer_expert_out):
    for l in range(DEPTH):
        ada = jax.nn.silu(c) @ w_ada[l] + b_ada[l]
        shift1, scale1, gate1, shift2, scale2, gate2 = jnp.split(ada[:, None, :], 6, axis=-1)
        h = rmsnorm(x, g_pre_mix[l]) * (1.0 + scale1) + shift1
        y = token_mixer(h, positions, w_in[l], w_up_sb[l], w_up_sa[l], w_out[l])
        x = x + gate1 * rmsnorm(y, g_post_mix[l])
        h = rmsnorm(x, g_pre_ffn[l]) * (1.0 + scale2) + shift2
        y = peer_ffn(h, w_peer_query[l], peer_sub_keys[l], peer_expert_in[l], peer_expert_out[l])
        x = x + gate2 * rmsnorm(y, g_post_ffn[l])
    return x
```

```python
import functools
import math

import numpy as np
import jax
import jax.numpy as jnp
from jax import lax
from jax.experimental import pallas as pl
from jax.experimental.pallas import tpu as pltpu

HEAD_DIM = 64
N_HEADS_SB = 8
N_HEADS_SA = 8
N_KV_SA = 2
N_IDX_HEADS = 8
IDX_DIM = 64
TOPK_MAX = 256
ROPE_THETA = 500000.0
ROT_DIM = HEAD_DIM // 4
ROT_HALF = ROT_DIM // 2
PEER_HEADS = 8
PEER_NKEYS = 128
PEER_QDIM = 256
PEER_TOPK = 16
NORM_EPS = 1e-6

LANES = 128
NEG_BIG = -1e30
VMEM_LIMIT = 56 * 1024 * 1024

F32 = jnp.float32
BF16 = jnp.bfloat16


def _cparams(sem):
    return pltpu.CompilerParams(dimension_semantics=sem, vmem_limit_bytes=VMEM_LIMIT)


def _dot(a, b):
    return jnp.dot(a, b, preferred_element_type=F32)


def _dot_nt(a, b):
    return lax.dot_general(a, b, (((1,), (1,)), ((), ())), preferred_element_type=F32)


def _split(a):
    hi = a.astype(BF16)
    lo = (a - hi.astype(F32)).astype(BF16)
    return hi, lo


def _dot3(a, b):
    ah, al = _split(a)
    bh, bl = _split(b)
    return _dot(ah, bh) + (_dot(ah, bl) + _dot(al, bh))


def _rms(x, g):
    return x * lax.rsqrt(jnp.mean(x * x, axis=-1, keepdims=True) + NORM_EPS) * g


def _ada_kernel(c_ref, w_ref, b_ref, o_ref):
    c = c_ref[...]
    s = c / (1.0 + jnp.exp(-c))
    o_ref[...] = _dot3(s, w_ref[...]) + b_ref[...]


def _ada(c, w, b):
    bsz, d = c.shape
    n_out = w.shape[1]
    return pl.pallas_call(
        _ada_kernel,
        grid=(n_out // d,),
        in_specs=[
            pl.BlockSpec((bsz, d), lambda j: (0, 0)),
            pl.BlockSpec((d, d), lambda j: (0, j)),
            pl.BlockSpec((1, d), lambda j: (0, j)),
        ],
        out_specs=pl.BlockSpec((bsz, d), lambda j: (0, j)),
        out_shape=jax.ShapeDtypeStruct((bsz, n_out), F32),
        compiler_params=_cparams(("parallel",)),
        name="ada",
    )(c, w, b.reshape(1, n_out))


def _rope_rows(x, cos_t, sin_t):
    lane = lax.broadcasted_iota(jnp.int32, (1, LANES), 1) % HEAD_DIM
    c = jnp.where(lane < ROT_DIM, cos_t, 1.0)
    s_lo = jnp.where(lane < ROT_HALF, -sin_t, 0.0)
    s_hi = jnp.where((lane >= ROT_HALF) & (lane < ROT_DIM), sin_t, 0.0)
    outs = []
    for j in range(x.shape[1] // LANES):
        xb = x[:, j * LANES:(j + 1) * LANES]
        up = pltpu.roll(xb, LANES - ROT_HALF, 1)
        dn = pltpu.roll(xb, ROT_HALF, 1)
        outs.append(xb * c + up * s_lo + dn * s_hi)
    return jnp.concatenate(outs, axis=1) if len(outs) > 1 else outs[0]


def _rope_cols(x, cos_t, sin_t):
    outs = []
    for j in range(x.shape[0] // HEAD_DIM):
        blk = x[j * HEAD_DIM:(j + 1) * HEAD_DIM]
        x1 = blk[0:ROT_HALF]
        x2 = blk[ROT_HALF:ROT_DIM]
        outs += [x1 * cos_t - x2 * sin_t, x2 * cos_t + x1 * sin_t, blk[ROT_DIM:]]
    return jnp.concatenate(outs, axis=0)


def _inproj_kernel(x_ref, posc_ref, posr_ref, shift_ref, scale_ref, g_ref, invl_ref, invc_ref,
                   wq_sb, wkt_sb, wv_sb, wq_sa, wkt_sa, wv_sa, wq_ix, wkt_ix, w_aux, w_gate,
                   q_sb_o, kt_sb_o, v_sb_o, q_sa_o, kt_sa_o, v_sa_o, q_ix_o, kt_ix_o, wix_o, sg_o):
    x = x_ref[...]
    h = _rms(x, g_ref[...]) * (1.0 + scale_ref[...]) + shift_ref[...]
    hb = h.astype(BF16)

    ang_r = posc_ref[...].astype(F32) * invl_ref[...]
    cos_r, sin_r = jnp.cos(ang_r), jnp.sin(ang_r)
    ang_c = posr_ref[...].astype(F32) * invc_ref[...]
    cos_c, sin_c = jnp.cos(ang_c), jnp.sin(ang_c)

    scale = HEAD_DIM ** -0.5
    q_sb_o[...] = (_dot(hb, wq_sb[...]) * scale).astype(BF16)
    kt_sb_o[...] = _dot_nt(wkt_sb[...], hb).astype(BF16)
    v_sb_o[...] = _dot(hb, wv_sb[...]).astype(BF16)
    q_sa_o[...] = (_rope_rows(_dot(hb, wq_sa[...]), cos_r, sin_r) * scale).astype(BF16)
    kt_sa_o[...] = _rope_cols(_dot_nt(wkt_sa[...], hb), cos_c, sin_c).astype(BF16)
    v_sa_o[...] = _dot(hb, wv_sa[...]).astype(BF16)
    q_ix_o[...] = _rope_rows(_dot(hb, wq_ix[...]), cos_r, sin_r).astype(BF16)
    kt_ix_o[...] = _rope_cols(_dot_nt(wkt_ix[...], hb), cos_c, sin_c).astype(BF16)
    wix_o[...] = _dot(hb, w_aux[...]) * (IDX_DIM ** -0.5 * N_IDX_HEADS ** -0.5)
    gate = _dot(hb, w_gate[...])
    sg_o[...] = (1.0 / (1.0 + jnp.exp(-gate))).astype(BF16)


def _inproj(x2, pos, shift_scale, g_pre, w_in, bsz, seq, tm):
    n, d = x2.shape
    w_sb = N_HEADS_SB * HEAD_DIM
    w_sa = N_HEADS_SA * HEAD_DIM
    w_kv = N_KV_SA * HEAD_DIM
    w_ixq = N_IDX_HEADS * IDX_DIM
    cuts = np.cumsum([w_sb, w_sb, w_sb, w_sa, w_kv, w_kv, w_ixq, IDX_DIM, N_IDX_HEADS, d, d])
    c = [0] + [int(v) for v in cuts]
    wb = w_in.astype(BF16)
    seg = lambda i: wb[:, c[i]:c[i + 1]]
    wq_sb, wk_sb, wv_sb, wq_sa, wk_sa, wv_sa, wq_ix, wk_ix, w_ixw = [seg(i) for i in range(9)]
    w_gate = wb[:, c[9]:c[11]]
    dup = lambda w: jnp.concatenate(
        [w[:, j * HEAD_DIM:(j + 1) * HEAD_DIM] for j in range(w.shape[1] // HEAD_DIM) for _ in (0, 1)], axis=1)
    wkt_sb = wk_sb.T
    wkt_sa = dup(wk_sa).T
    wv_sa2 = dup(wv_sa)
    wkt_ix = dup(wk_ix).T
    w_aux = jnp.concatenate([w_ixw, jnp.zeros((d, LANES - N_IDX_HEADS), BF16)], axis=1)

    inv_freq = jnp.power(ROPE_THETA, -2.0 * jnp.arange(ROT_HALF, dtype=F32) / ROT_DIM)
    inv_lane = jnp.tile(inv_freq, LANES // ROT_HALF).reshape(1, LANES)
    inv_col = jnp.broadcast_to(inv_freq[:, None], (ROT_HALF, tm))

    nt = seq // tm
    row = lambda w: pl.BlockSpec((tm, w), lambda i: (i, 0))
    colT = lambda r: pl.BlockSpec((None, r, tm), lambda i: (i // nt, 0, i % nt))
    const = lambda a: pl.BlockSpec(a.shape, lambda i: (0,) * a.ndim)
    weights = [wq_sb, wkt_sb, wv_sb, wq_sa, wkt_sa, wv_sa2, wq_ix, wkt_ix, w_aux, w_gate]
    out_shape = [
        jax.ShapeDtypeStruct((n, w_sb), BF16),
        jax.ShapeDtypeStruct((bsz, w_sb, seq), BF16),
        jax.ShapeDtypeStruct((n, w_sb), BF16),
        jax.ShapeDtypeStruct((n, w_sa), BF16),
        jax.ShapeDtypeStruct((bsz, 2 * w_kv, seq), BF16),
        jax.ShapeDtypeStruct((n, 2 * w_kv), BF16),
        jax.ShapeDtypeStruct((n, w_ixq), BF16),
        jax.ShapeDtypeStruct((bsz, 2 * IDX_DIM, seq), BF16),
        jax.ShapeDtypeStruct((n, LANES), F32),
        jax.ShapeDtypeStruct((n, 2 * d), BF16),
    ]
    out_specs = [row(w_sb), colT(w_sb), row(w_sb), row(w_sa), colT(2 * w_kv), row(2 * w_kv),
                 row(w_ixq), colT(2 * IDX_DIM), row(LANES), row(2 * d)]
    return pl.pallas_call(
        _inproj_kernel,
        grid=(n // tm,),
        in_specs=[
            row(d),
            pl.BlockSpec((tm, 1), lambda i: (i, 0)),
            pl.BlockSpec((None, 1, tm), lambda i: (i // nt, 0, i % nt)),
            pl.BlockSpec((None, 1, d), lambda i: (i // nt, 0, 0)),
            pl.BlockSpec((None, 1, d), lambda i: (i // nt, 0, 1)),
            const(g_pre), const(inv_lane), const(inv_col),
        ] + [const(w) for w in weights],
        out_specs=out_specs,
        out_shape=out_shape,
        compiler_params=_cparams(("parallel",)),
        name="inproj",
    )(x2, pos.reshape(n, 1), pos.reshape(bsz, 1, seq), shift_scale, shift_scale, g_pre,
      inv_lane, inv_col, *weights)


def _softplus(z):
    return jnp.maximum(z, 0.0) + jnp.log1p(jnp.exp(-jnp.abs(z)))


def _sb_kernel(q_ref, kt_ref, v_ref, o_ref, *, tq):
    qi = pl.program_id(2)
    q2 = q_ref[...]
    lane_half = lax.broadcasted_iota(jnp.int32, (1, LANES), 1) // HEAD_DIM
    r_i = lax.broadcasted_iota(jnp.int32, (tq, tq), 0)
    c_i = lax.broadcasted_iota(jnp.int32, (tq, tq), 1)
    later = jnp.where(r_i > c_i, 1.0, 0.0).astype(BF16)
    strict = c_i < r_i

    def block(qm, kb, carry, acc, diag):
        k0 = pl.multiple_of(kb * tq, tq)
        z = _dot(qm, kt_ref[:, pl.ds(k0, tq)])
        sp = _softplus(z)
        lf = -sp
        if diag:
            lf = jnp.where(strict, lf, 0.0)
        lf_hi, lf_lo = _split(lf)
        between = _dot(lf_hi, later) + _dot(lf_lo, later) + carry
        a = jnp.exp((z - sp) + between)
        if diag:
            a = jnp.where(strict, a, 0.0)
        acc = acc + _dot(a.astype(BF16), v_ref[pl.ds(k0, tq), :])
        carry = carry + jnp.sum(lf, axis=1, keepdims=True)
        return carry, acc

    outs = []
    for hh in (0, 1):
        qm = jnp.where(lane_half == hh, q2, jnp.zeros_like(q2))
        carry0 = jnp.zeros((tq, 1), F32)
        acc0 = jnp.zeros((tq, LANES), F32)
        carry, acc = block(qm, qi, carry0, acc0, True)

        def body(it, st, qm=qm):
            return block(qm, qi - 1 - it, st[0], st[1], False)

        carry, acc = lax.fori_loop(0, qi, body, (carry, acc))
        outs.append(acc)
    o_ref[...] = jnp.where(lane_half == 0, outs[0], outs[1]).astype(o_ref.dtype)


def _sb_attn(q, kt, v, bsz, seq, tq):
    n, w = q.shape
    npair = w // LANES
    nq = seq // tq
    return pl.pallas_call(
        functools.partial(_sb_kernel, tq=tq),
        grid=(bsz, npair, nq),
        in_specs=[
            pl.BlockSpec((tq, LANES), lambda b, p, i: (b * nq + i, p)),
            pl.BlockSpec((None, LANES, seq), lambda b, p, i: (b, p, 0)),
            pl.BlockSpec((seq, LANES), lambda b, p, i: (b, p)),
        ],
        out_specs=pl.BlockSpec((tq, LANES), lambda b, p, i: (b * nq + i, p)),
        out_shape=jax.ShapeDtypeStruct((n, w), BF16),
        compiler_params=_cparams(("parallel", "parallel", "arbitrary")),
        name="sb_attn",
    )(q, kt, v)


def _sort_key(s):
    bits = lax.bitcast_convert_type(s, jnp.int32)
    bits = jnp.where(bits == jnp.int32(-2 ** 31), 0, bits)
    return bits ^ ((bits >> 31) & jnp.int32(0x7FFFFFFF))


def _dsa_kernel(qix_ref, ktix_ref, wix_ref, q_ref, kt_ref, v_ref, o_ref,
                qm_scr, key_scr, bias_scr, qs_scr, m_scr, acc_scr, *, tq, k_sel):
    qi = pl.program_id(1)
    nkb = qi + 1
    lane_half = lax.broadcasted_iota(jnp.int32, (1, LANES), 1) // HEAD_DIM
    r_i = lax.broadcasted_iota(jnp.int32, (tq, tq), 0)
    c_i = lax.broadcasted_iota(jnp.int32, (tq, tq), 1)
    upto = jnp.where(r_i <= c_i, 1.0, 0.0).astype(BF16)

    for h in range(N_IDX_HEADS):
        qp = qix_ref[:, (h // 2) * LANES:(h // 2 + 1) * LANES]
        qm_scr[h] = jnp.where(lane_half == h % 2, qp, jnp.zeros_like(qp))

    def score_body(kb, _):
        k0 = pl.multiple_of(kb * tq, tq)
        kk = ktix_ref[:, pl.ds(k0, tq)]
        sc = jnp.zeros((tq, tq), F32)
        for h in range(N_IDX_HEADS):
            rel = jnp.maximum(_dot(qm_scr[h], kk), 0.0)
            sc = sc + rel * wix_ref[:, h:h + 1]
        causal = (c_i + k0) <= (r_i + qi * tq)
        sc = jnp.where(causal, sc, -jnp.inf)
        key_scr[:, pl.ds(k0, tq)] = _sort_key(sc)
        return 0

    lax.fori_loop(0, nkb, score_body, 0)

    def count_ge(cand):
        def body(kb, part):
            k0 = pl.multiple_of(kb * tq, tq)
            ge = jnp.where(key_scr[:, pl.ds(k0, tq)] >= cand, 1, 0)
            for j in range(tq // LANES):
                part = part + ge[:, j * LANES:(j + 1) * LANES]
            return part
        part = lax.fori_loop(0, nkb, body, jnp.zeros((tq, LANES), jnp.int32))
        return jnp.sum(part, axis=1, keepdims=True)

    def bit_body(it, cur):
        cand = cur + (jnp.int32(1) << (31 - it))
        return jnp.where(count_ge(cand) >= k_sel, cand, cur)

    tau = lax.fori_loop(0, 32, bit_body, jnp.full((tq, 1), -2 ** 31, jnp.int32))
    n_gt = count_ge(tau + 1)
    need = (k_sel - n_gt).astype(F32)

    def bias_body(kb, seen):
        k0 = pl.multiple_of(kb * tq, tq)
        key = key_scr[:, pl.ds(k0, tq)]
        tie = key == tau
        rank = _dot(jnp.where(tie, 1.0, 0.0).astype(BF16), upto) + seen
        sel = (key > tau) | (tie & (rank <= need))
        causal = (c_i + k0) <= (r_i + qi * tq)
        bias_scr[:, pl.ds(k0, tq)] = jnp.where(sel & causal, 0.0, NEG_BIG)
        return rank[:, tq - 1:tq]

    lax.fori_loop(0, nkb, bias_body, jnp.zeros((tq, 1), F32))

    gsz = N_HEADS_SA // N_KV_SA
    ones = jnp.ones((tq, LANES), BF16)
    for g in range(N_KV_SA):
        for hh in range(gsz):
            h = g * gsz + hh
            qp = q_ref[:, (h // 2) * LANES:(h // 2 + 1) * LANES]
            qs_scr[hh * tq:(hh + 1) * tq, :] = jnp.where(lane_half == h % 2, qp, jnp.zeros_like(qp))
        m_scr[...] = jnp.full(m_scr.shape, -jnp.inf, F32)
        acc_scr[...] = jnp.zeros(acc_scr.shape, F32)

        def attn_body(kb, _, g=g):
            k0 = pl.multiple_of(kb * tq, tq)
            kk = kt_ref[g * LANES:(g + 1) * LANES, pl.ds(k0, tq)]
            logits = _dot(qs_scr[...], kk)
            logits = (logits.reshape(gsz, tq, tq) + bias_scr[:, pl.ds(k0, tq)][None]).reshape(gsz * tq, tq)
            m_old = m_scr[...]
            m_new = jnp.maximum(m_old, jnp.max(logits, axis=1, keepdims=True))
            alpha = jnp.exp(m_old - m_new)
            p = jnp.exp(logits - m_new).astype(BF16)
            vv = jnp.concatenate([v_ref[pl.ds(k0, tq), g * LANES:(g + 1) * LANES], ones], axis=1)
            acc_scr[...] = alpha * acc_scr[...] + _dot(p, vv)
            m_scr[...] = m_new
            return 0

        lax.fori_loop(0, nkb, attn_body, 0)
        for hp in range(gsz // 2):
            pair = []
            for hh in (2 * hp, 2 * hp + 1):
                a = acc_scr[hh * tq:(hh + 1) * tq, :]
                pair.append(a[:, :LANES] / a[:, LANES:LANES + 1])
            col = (g * gsz // 2 + hp) * LANES
            o_ref[:, col:col + LANES] = jnp.where(lane_half == 0, pair[0], pair[1]).astype(o_ref.dtype)


def _dsa_attn(q_ix, kt_ix, w_ix, q, kt, v, bsz, seq, tq):
    n, w = q.shape
    nq = seq // tq
    k_sel = min(TOPK_MAX, seq // 4)
    gsz = N_HEADS_SA // N_KV_SA
    rowb = lambda width: pl.BlockSpec((tq, width), lambda b, i: (b * nq + i, 0))
    return pl.pallas_call(
        functools.partial(_dsa_kernel, tq=tq, k_sel=k_sel),
        grid=(bsz, nq),
        in_specs=[
            rowb(q_ix.shape[1]),
            pl.BlockSpec((None, kt_ix.shape[1], seq), lambda b, i: (b, 0, 0)),
            rowb(LANES),
            rowb(w),
            pl.BlockSpec((None, kt.shape[1], seq), lambda b, i: (b, 0, 0)),
            pl.BlockSpec((seq, v.shape[1]), lambda b, i: (b, 0)),
        ],
        out_specs=rowb(w),
        out_shape=jax.ShapeDtypeStruct((n, w), BF16),
        scratch_shapes=[
            pltpu.VMEM((N_IDX_HEADS, tq, LANES), BF16),
            pltpu.VMEM((tq, seq), jnp.int32),
            pltpu.VMEM((tq, seq), F32),
            pltpu.VMEM((gsz * tq, LANES), BF16),
            pltpu.VMEM((gsz * tq, 1), F32),
            pltpu.VMEM((gsz * tq, 2 * LANES), F32),
        ],
        compiler_params=_cparams(("parallel", "arbitrary")),
        name="dsa_attn",
    )(q_ix, kt_ix, w_ix, q, kt, v)


def _put_row(stack, r, row):
    idx = lax.broadcasted_iota(jnp.int32, stack.shape, 0)
    return jnp.where(idx == r, row, stack)


def _top_desc(x, k):
    out = jnp.zeros((k, x.shape[1]), F32)
    for r in range(k):
        mx = jnp.max(x, axis=0, keepdims=True)
        out = _put_row(out, r, mx)
        x = jnp.where(x == mx, -jnp.inf, x)
    return out


def _merge_kernel(osb_ref, osa_ref, sg_ref, x_ref, gate1_ref, shift2_ref, scale2_ref,
                  gpost_ref, gpre_ref, wup_sb, wup_sa, wout, wqt, keys_ref,
                  x1_o, h2t_o, s_o, st_o):
    d = x_ref.shape[1]
    y_sb = _dot(osb_ref[...], wup_sb[...])
    y_sa = _dot(osa_ref[...], wup_sa[...])
    sg = sg_ref[...].astype(F32)
    merged = sg[:, :d] * y_sb + sg[:, d:] * y_sa
    y = _dot(merged.astype(BF16), wout[...])
    x1 = x_ref[...] + gate1_ref[...] * _rms(y, gpost_ref[...])
    x1_o[...] = x1
    h2 = _rms(x1, gpre_ref[...]) * (1.0 + scale2_ref[...]) + shift2_ref[...]
    h2t = h2.T.astype(BF16)
    h2t_o[...] = h2t
    qt = _dot(wqt[...], h2t)

    half = PEER_QDIM // 2
    stats = jnp.zeros(st_o.shape, F32)
    for h in range(PEER_HEADS):
        tops = []
        for p in (0, 1):
            r0 = (2 * h + p) * half
            q_hi, q_lo = _split(qt[r0:r0 + half])
            kf = keys_ref[2 * h + p]
            k_hi, k_lo = _split(kf)
            s = _dot(k_hi, q_hi) + (_dot(k_hi, q_lo) + _dot(k_lo, q_hi))
            s_o[2 * h + p] = s
            tops.append(_top_desc(s, PEER_TOPK))
        cand = jnp.concatenate([tops[0][k1:k1 + 1] + tops[1] for k1 in range(PEER_TOPK)], axis=0)
        best = _top_desc(cand, PEER_TOPK)
        mx = best[0:1]
        z = jnp.sum(jnp.exp(best - mx), axis=0, keepdims=True)
        stats = _put_row(stats, h, best[PEER_TOPK - 1:PEER_TOPK])
        stats = _put_row(stats, PEER_HEADS + h, mx)
        stats = _put_row(stats, 2 * PEER_HEADS + h, 1.0 / z)
    st_o[...] = stats


def _merge(o_sb, o_sa, sg, x2, ada3, g_post, g_pre, w_up_sb, w_up_sa, w_out, w_query, sub_keys,
           seq, tm):
    n, d = x2.shape
    nt = seq // tm
    nk = 2 * PEER_HEADS
    wqt = w_query.T.astype(BF16)
    keys = sub_keys.reshape(nk, PEER_NKEYS, PEER_QDIM // 2)
    wup_sb, wup_sa, wout = w_up_sb.astype(BF16), w_up_sa.astype(BF16), w_out.astype(BF16)
    row = lambda w: pl.BlockSpec((tm, w), lambda i: (i, 0))
    const = lambda a: pl.BlockSpec(a.shape, lambda i: (0,) * a.ndim)
    adab = lambda j: pl.BlockSpec((None, 1, d), lambda i: (i // nt, 0, j))
    return pl.pallas_call(
        _merge_kernel,
        grid=(n // tm,),
        in_specs=[row(o_sb.shape[1]), row(o_sa.shape[1]), row(2 * d), row(d),
                  adab(2), adab(3), adab(4), const(g_post), const(g_pre),
                  const(wup_sb), const(wup_sa), const(wout), const(wqt), const(keys)],
        out_specs=[row(d),
                   pl.BlockSpec((d, tm), lambda i: (0, i)),
                   pl.BlockSpec((nk, PEER_NKEYS, tm), lambda i: (0, 0, i)),
                   pl.BlockSpec((4 * PEER_HEADS, tm), lambda i: (0, i))],
        out_shape=[jax.ShapeDtypeStruct((n, d), F32),
                   jax.ShapeDtypeStruct((d, n), BF16),
                   jax.ShapeDtypeStruct((nk, PEER_NKEYS, n), F32),
                   jax.ShapeDtypeStruct((4 * PEER_HEADS, n), F32)],
        compiler_params=_cparams(("parallel",)),
        name="merge",
    )(o_sb, o_sa, sg, x2, ada3, ada3, ada3, g_post, g_pre, wup_sb, wup_sa, wout, wqt, keys)


def _erf(x):
    ax = jnp.abs(x)
    t = 1.0 / (1.0 + 0.3275911 * ax)
    poly = t * (0.254829592 + t * (-0.284496736 + t * (1.421413741 + t * (-1.453152027 + t * 1.061405429))))
    e = 1.0 - poly * jnp.exp(-ax * ax)
    return jnp.where(x < 0, -e, e)


def _peer_kernel(h2t_ref, win_ref, woutt_ref, s_ref, st_ref, y_o, acc_scr, act_scr, *, te):
    k = pl.program_id(1)

    @pl.when(k == 0)
    def _():
        acc_scr[...] = jnp.zeros(acc_scr.shape, F32)

    a = _dot(win_ref[...], h2t_ref[...])
    gel = 0.5 * a * (1.0 + _erf(a * (2.0 ** -0.5)))
    nk = PEER_NKEYS
    for ii in range(te // nk):
        i = k * (te // nk) + ii
        gate = jnp.zeros((nk, a.shape[1]), F32)
        for h in range(PEER_HEADS):
            tot = s_ref[2 * h, pl.ds(i, 1), :] + s_ref[2 * h + 1]
            tau = st_ref[h:h + 1, :]
            mx = st_ref[PEER_HEADS + h:PEER_HEADS + h + 1, :]
            iz = st_ref[2 * PEER_HEADS + h:2 * PEER_HEADS + h + 1, :]
            gate = gate + jnp.where(tot >= tau, jnp.exp(tot - mx) * iz, 0.0)
        act_scr[ii * nk:(ii + 1) * nk, :] = (gate * gel[ii * nk:(ii + 1) * nk]).astype(BF16)
    acc_scr[...] += _dot(woutt_ref[...], act_scr[...])

    @pl.when(k == pl.num_programs(1) - 1)
    def _():
        y_o[...] = acc_scr[...].T


def _peer(h2t, w_in_e, w_out_e, s_all, stats, tm, te):
    d, n = h2t.shape
    n_exp = w_in_e.shape[0]
    win = w_in_e.astype(BF16)
    woutt = w_out_e.T.astype(BF16)
    return pl.pallas_call(
        functools.partial(_peer_kernel, te=te),
        grid=(n // tm, n_exp // te),
        in_specs=[
            pl.BlockSpec((d, tm), lambda i, k: (0, i)),
            pl.BlockSpec((te, d), lambda i, k: (k, 0)),
            pl.BlockSpec((d, te), lambda i, k: (0, k)),
            pl.BlockSpec((s_all.shape[0], PEER_NKEYS, tm), lambda i, k: (0, 0, i)),
            pl.BlockSpec((stats.shape[0], tm), lambda i, k: (0, i)),
        ],
        out_specs=pl.BlockSpec((tm, d), lambda i, k: (i, 0)),
        out_shape=jax.ShapeDtypeStruct((n, d), F32),
        scratch_shapes=[pltpu.VMEM((d, tm), F32), pltpu.VMEM((te, tm), BF16)],
        compiler_params=_cparams(("parallel", "arbitrary")),
        name="peer",
    )(h2t, win, woutt, s_all, stats)


def _final_kernel(x1_ref, y_ref, gate_ref, g_ref, o_ref):
    o_ref[...] = x1_ref[...] + gate_ref[...] * _rms(y_ref[...], g_ref[...])


def _final(x1, y, ada3, g_post, seq, tm):
    n, d = x1.shape
    nt = seq // tm
    row = pl.BlockSpec((tm, d), lambda i: (i, 0))
    return pl.pallas_call(
        _final_kernel,
        grid=(n // tm,),
        in_specs=[row, row, pl.BlockSpec((None, 1, d), lambda i: (i // nt, 0, 5)),
                  pl.BlockSpec((1, d), lambda i: (0, 0))],
        out_specs=row,
        out_shape=jax.ShapeDtypeStruct((n, d), F32),
        compiler_params=_cparams(("parallel",)),
        name="final",
    )(x1, y, ada3, g_post)


def _layer(x2, c, pos, bsz, seq, w_ada, b_ada, g_pre_mix, g_post_mix, w_in, w_up_sb, w_up_sa, w_out,
           g_pre_ffn, g_post_ffn, w_peer_query, peer_sub_keys, peer_expert_in, peer_expert_out):
    n, d = x2.shape
    tq = 256 if seq % 256 == 0 else 128
    tm = 256
    row1 = lambda g: g.reshape(1, d)
    ada3 = _ada(c, w_ada, b_ada).reshape(bsz, 1, 6 * d)
    (q_sb, kt_sb, v_sb, q_sa, kt_sa, v_sa, q_ix, kt_ix, w_ix, sg) = _inproj(
        x2, pos, ada3, row1(g_pre_mix), w_in, bsz, seq, tm)
    o_sb = _sb_attn(q_sb, kt_sb, v_sb, bsz, seq, tq)
    o_sa = _dsa_attn(q_ix, kt_ix, w_ix, q_sa, kt_sa, v_sa, bsz, seq, tq)
    x1, h2t, s_all, stats = _merge(o_sb, o_sa, sg, x2, ada3, row1(g_post_mix), row1(g_pre_ffn),
                                   w_up_sb, w_up_sa, w_out, w_peer_query, peer_sub_keys, seq, tm)
    y = _peer(h2t, peer_expert_in, peer_expert_out, s_all, stats, tm=min(1024, n), te=512)
    return _final(x1, y, ada3, row1(g_post_ffn), seq, tm)


def kernel(x, c, positions, w_ada, b_ada, g_pre_mix, g_post_mix, w_in, w_up_sb, w_up_sa, w_out,
           g_pre_ffn, g_post_ffn, w_peer_query, peer_sub_keys, peer_expert_in, peer_expert_out):
    bsz, seq, d = x.shape
    x2 = x.reshape(bsz * seq, d)
    for l in range(w_ada.shape[0]):
        x2 = _layer(x2, c, positions, bsz, seq, w_ada[l], b_ada[l], g_pre_mix[l], g_post_mix[l],
                    w_in[l], w_up_sb[l], w_up_sa[l], w_out[l], g_pre_ffn[l], g_post_ffn[l],
                    w_peer_query[l], peer_sub_keys[l], peer_expert_in[l], peer_expert_out[l])
    return x2.reshape(bsz, seq, d)
```

```python
import functools
import math

import numpy as np
import jax
import jax.numpy as jnp
from jax import lax
from jax.experimental import pallas as pl
from jax.experimental.pallas import tpu as pltpu

HEAD_DIM = 64
N_HEADS_SB = 8
N_HEADS_SA = 8
N_KV_SA = 2
N_IDX_HEADS = 8
IDX_DIM = 64
TOPK_MAX = 256
ROPE_THETA = 500000.0
ROT_DIM = HEAD_DIM // 4
ROT_HALF = ROT_DIM // 2
PEER_HEADS = 8
PEER_NKEYS = 128
PEER_QDIM = 256
PEER_TOPK = 16
NORM_EPS = 1e-6

LANES = 128
NEG_BIG = -1e30
VMEM_LIMIT = 56 * 1024 * 1024

F32 = jnp.float32
BF16 = jnp.bfloat16


def _cparams(sem):
    return pltpu.CompilerParams(dimension_semantics=sem, vmem_limit_bytes=VMEM_LIMIT)


def _dot(a, b):
    return jnp.dot(a, b, preferred_element_type=F32)


def _dot_nt(a, b):
    return lax.dot_general(a, b, (((1,), (1,)), ((), ())), preferred_element_type=F32)


def _split(a):
    hi = a.astype(BF16)
    lo = (a - hi.astype(F32)).astype(BF16)
    return hi, lo


def _dot3(a, b):
    ah, al = _split(a)
    bh, bl = _split(b)
    return _dot(ah, bh) + (_dot(ah, bl) + _dot(al, bh))


def _rms(x, g):
    return x * lax.rsqrt(jnp.mean(x * x, axis=-1, keepdims=True) + NORM_EPS) * g


def _ada_kernel(c_ref, w_ref, b_ref, o_ref):
    c = c_ref[...]
    s = c / (1.0 + jnp.exp(-c))
    o_ref[...] = _dot3(s, w_ref[...]) + b_ref[...]


def _ada(c, w, b):
    bsz, d = c.shape
    n_out = w.shape[1]
    return pl.pallas_call(
        _ada_kernel,
        grid=(n_out // d,),
        in_specs=[
            pl.BlockSpec((bsz, d), lambda j: (0, 0)),
            pl.BlockSpec((d, d), lambda j: (0, j)),
            pl.BlockSpec((1, d), lambda j: (0, j)),
        ],
        out_specs=pl.BlockSpec((bsz, d), lambda j: (0, j)),
        out_shape=jax.ShapeDtypeStruct((bsz, n_out), F32),
        compiler_params=_cparams(("parallel",)),
        name="ada",
    )(c, w, b.reshape(1, n_out))


def _rope_rows(x, cos_t, sin_t):
    lane = lax.broadcasted_iota(jnp.int32, (1, LANES), 1) % HEAD_DIM
    c = jnp.where(lane < ROT_DIM, cos_t, 1.0)
    s_lo = jnp.where(lane < ROT_HALF, -sin_t, 0.0)
    s_hi = jnp.where((lane >= ROT_HALF) & (lane < ROT_DIM), sin_t, 0.0)
    outs = []
    for j in range(x.shape[1] // LANES):
        xb = x[:, j * LANES:(j + 1) * LANES]
        up = pltpu.roll(xb, LANES - ROT_HALF, 1)
        dn = pltpu.roll(xb, ROT_HALF, 1)
        outs.append(xb * c + up * s_lo + dn * s_hi)
    return jnp.concatenate(outs, axis=1) if len(outs) > 1 else outs[0]


def _rope_cols(x, cos_t, sin_t):
    outs = []
    for j in range(x.shape[0] // HEAD_DIM):
        blk = x[j * HEAD_DIM:(j + 1) * HEAD_DIM]
        x1 = blk[0:ROT_HALF]
        x2 = blk[ROT_HALF:ROT_DIM]
        outs += [x1 * cos_t - x2 * sin_t, x2 * cos_t + x1 * sin_t, blk[ROT_DIM:]]
    return jnp.concatenate(outs, axis=0)


def _inproj_kernel(x_ref, posc_ref, posr_ref, shift_ref, scale_ref, g_ref, invl_ref, invc_ref,
                   wq_sb, wkt_sb, wv_sb, wq_sa, wkt_sa, wv_sa, wq_ix, wkt_ix, w_aux, w_gate,
                   q_sb_o, kt_sb_o, v_sb_o, q_sa_o, kt_sa_o, v_sa_o, q_ix_o, kt_ix_o, wix_o, sg_o):
    x = x_ref[...]
    h = _rms(x, g_ref[...]) * (1.0 + scale_ref[...]) + shift_ref[...]
    hb = h.astype(BF16)

    ang_r = posc_ref[...].astype(F32) * invl_ref[...]
    cos_r, sin_r = jnp.cos(ang_r), jnp.sin(ang_r)
    ang_c = posr_ref[...].astype(F32) * invc_ref[...]
    cos_c, sin_c = jnp.cos(ang_c), jnp.sin(ang_c)

    scale = HEAD_DIM ** -0.5
    q_sb_o[...] = (_dot(hb, wq_sb[...]) * scale).astype(BF16)
    kt_sb_o[...] = _dot_nt(wkt_sb[...], hb).astype(BF16)
    v_sb_o[...] = _dot(hb, wv_sb[...]).astype(BF16)
    q_sa_o[...] = (_rope_rows(_dot(hb, wq_sa[...]), cos_r, sin_r) * scale).astype(BF16)
    kt_sa_o[...] = _rope_cols(_dot_nt(wkt_sa[...], hb), cos_c, sin_c).astype(BF16)
    v_sa_o[...] = _dot(hb, wv_sa[...]).astype(BF16)
    q_ix_o[...] = _rope_rows(_dot(hb, wq_ix[...]), cos_r, sin_r).astype(BF16)
    kt_ix_o[...] = _rope_cols(_dot_nt(wkt_ix[...], hb), cos_c, sin_c).astype(BF16)
    wix_o[...] = _dot(hb, w_aux[...]) * (IDX_DIM ** -0.5 * N_IDX_HEADS ** -0.5)
    gate = _dot(hb, w_gate[...])
    sg_o[...] = (1.0 / (1.0 + jnp.exp(-gate))).astype(BF16)


def _inproj(x2, pos, shift_scale, g_pre, w_in, bsz, seq, tm):
    n, d = x2.shape
    w_sb = N_HEADS_SB * HEAD_DIM
    w_sa = N_HEADS_SA * HEAD_DIM
    w_kv = N_KV_SA * HEAD_DIM
    w_ixq = N_IDX_HEADS * IDX_DIM
    cuts = np.cumsum([w_sb, w_sb, w_sb, w_sa, w_kv, w_kv, w_ixq, IDX_DIM, N_IDX_HEADS, d, d])
    c = [0] + [int(v) for v in cuts]
    wb = w_in.astype(BF16)
    seg = lambda i: wb[:, c[i]:c[i + 1]]
    wq_sb, wk_sb, wv_sb, wq_sa, wk_sa, wv_sa, wq_ix, wk_ix, w_ixw = [seg(i) for i in range(9)]
    w_gate = wb[:, c[9]:c[11]]
    dup = lambda w: jnp.concatenate(
        [w[:, j * HEAD_DIM:(j + 1) * HEAD_DIM] for j in range(w.shape[1] // HEAD_DIM) for _ in (0, 1)], axis=1)
    wkt_sb = wk_sb.T
    wkt_sa = dup(wk_sa).T
    wv_sa2 = dup(wv_sa)
    wkt_ix = dup(wk_ix).T
    w_aux = jnp.concatenate([w_ixw, jnp.zeros((d, LANES - N_IDX_HEADS), BF16)], axis=1)

    inv_freq = jnp.power(ROPE_THETA, -2.0 * jnp.arange(ROT_HALF, dtype=F32) / ROT_DIM)
    inv_lane = jnp.tile(inv_freq, LANES // ROT_HALF).reshape(1, LANES)
    inv_col = jnp.broadcast_to(inv_freq[:, None], (ROT_HALF, tm))

    nt = seq // tm
    row = lambda w: pl.BlockSpec((tm, w), lambda i: (i, 0))
    colT = lambda r: pl.BlockSpec((None, r, tm), lambda i: (i // nt, 0, i % nt))
    const = lambda a: pl.BlockSpec(a.shape, lambda i: (0,) * a.ndim)
    weights = [wq_sb, wkt_sb, wv_sb, wq_sa, wkt_sa, wv_sa2, wq_ix, wkt_ix, w_aux, w_gate]
    out_shape = [
        jax.ShapeDtypeStruct((n, w_sb), BF16),
        jax.ShapeDtypeStruct((bsz, w_sb, seq), BF16),
        jax.ShapeDtypeStruct((n, w_sb), BF16),
        jax.ShapeDtypeStruct((n, w_sa), BF16),
        jax.ShapeDtypeStruct((bsz, 2 * w_kv, seq), BF16),
        jax.ShapeDtypeStruct((n, 2 * w_kv), BF16),
        jax.ShapeDtypeStruct((n, w_ixq), BF16),
        jax.ShapeDtypeStruct((bsz, 2 * IDX_DIM, seq), BF16),
        jax.ShapeDtypeStruct((n, LANES), F32),
        jax.ShapeDtypeStruct((n, 2 * d), BF16),
    ]
    out_specs = [row(w_sb), colT(w_sb), row(w_sb), row(w_sa), colT(2 * w_kv), row(2 * w_kv),
                 row(w_ixq), colT(2 * IDX_DIM), row(LANES), row(2 * d)]
    return pl.pallas_call(
        _inproj_kernel,
        grid=(n // tm,),
        in_specs=[
            row(d),
            pl.BlockSpec((tm, 1), lambda i: (i, 0)),
            pl.BlockSpec((None, 1, tm), lambda i: (i // nt, 0, i % nt)),
            pl.BlockSpec((None, 1, d), lambda i: (i // nt, 0, 0)),
            pl.BlockSpec((None, 1, d), lambda i: (i // nt, 0, 1)),
            const(g_pre), const(inv_lane), const(inv_col),
        ] + [const(w) for w in weights],
        out_specs=out_specs,
        out_shape=out_shape,
        compiler_params=_cparams(("parallel",)),
        name="inproj",
    )(x2, pos.reshape(n, 1), pos.reshape(bsz, 1, seq), shift_scale, shift_scale, g_pre,
      inv_lane, inv_col, *weights)


def _softplus(z):
    return jnp.maximum(z, 0.0) + jnp.log(1.0 + jnp.exp(-jnp.abs(z)))


def _sb_kernel(q_ref, kt_ref, v_ref, o_ref, *, tq):
    qi = pl.program_id(2)
    q2 = q_ref[...]
    lane_half = lax.broadcasted_iota(jnp.int32, (1, LANES), 1) // HEAD_DIM
    r_i = lax.broadcasted_iota(jnp.int32, (tq, tq), 0)
    c_i = lax.broadcasted_iota(jnp.int32, (tq, tq), 1)
    strict = c_i < r_i
    ck = min(tq, 2 * LANES)
    later = jnp.where(lax.broadcasted_iota(jnp.int32, (ck, ck), 0) > lax.broadcasted_iota(jnp.int32, (ck, ck), 1),
                      1.0, 0.0).astype(BF16)
    qms = [jnp.where(lane_half == hh, q2, jnp.zeros_like(q2)) for hh in (0, 1)]

    def block(kb, state, diag):
        k0 = pl.multiple_of(kb * tq, tq)
        kt = kt_ref[:, pl.ds(k0, tq)]
        v = v_ref[pl.ds(k0, tq), :]
        new = []
        for hh in (0, 1):
            carry, acc = state[2 * hh], state[2 * hh + 1]
            z = _dot(qms[hh], kt)
            sp = _softplus(z)
            lf = -sp
            if diag:
                lf = jnp.where(strict, lf, 0.0)
            chunks = [None] * (tq // ck)
            for c in reversed(range(tq // ck)):
                lf_c = lf[:, c * ck:(c + 1) * ck]
                lf_hi, lf_lo = _split(lf_c)
                chunks[c] = _dot(lf_hi, later) + _dot(lf_lo, later) + carry
                carry = carry + jnp.sum(lf_c, axis=1, keepdims=True)
            between = jnp.concatenate(chunks, axis=1) if len(chunks) > 1 else chunks[0]
            a = jnp.exp((z - sp) + between)
            if diag:
                a = jnp.where(strict, a, 0.0)
            new += [carry, acc + _dot(a.astype(BF16), v)]
        return tuple(new)

    zero = (jnp.zeros((tq, 1), F32), jnp.zeros((tq, LANES), F32))
    state = block(qi, zero + zero, True)
    state = lax.fori_loop(0, qi, lambda it, st: block(qi - 1 - it, st, False), state)
    o_ref[...] = jnp.where(lane_half == 0, state[1], state[3]).astype(o_ref.dtype)


def _sb_attn(q, kt, v, bsz, seq, tq):
    n, w = q.shape
    npair = w // LANES
    nq = seq // tq
    return pl.pallas_call(
        functools.partial(_sb_kernel, tq=tq),
        grid=(bsz, npair, nq),
        in_specs=[
            pl.BlockSpec((tq, LANES), lambda b, p, i: (b * nq + i, p)),
            pl.BlockSpec((None, LANES, seq), lambda b, p, i: (b, p, 0)),
            pl.BlockSpec((seq, LANES), lambda b, p, i: (b, p)),
        ],
        out_specs=pl.BlockSpec((tq, LANES), lambda b, p, i: (b * nq + i, p)),
        out_shape=jax.ShapeDtypeStruct((n, w), BF16),
        compiler_params=_cparams(("parallel", "parallel", "arbitrary")),
        name="sb_attn",
    )(q, kt, v)


def _sort_key(s):
    bits = lax.bitcast_convert_type(s, jnp.int32)
    bits = jnp.where(bits == jnp.int32(-2 ** 31), 0, bits)
    return bits ^ ((bits >> 31) & jnp.int32(0x7FFFFFFF))


def _dsa_kernel(qix_ref, ktix_ref, wix_ref, q_ref, kt_ref, v_ref, o_ref,
                qm_scr, key_scr, bias_scr, qs_scr, m_scr, acc_scr, *, tq, k_sel):
    qi = pl.program_id(1)
    nkb = qi + 1
    lane_half = lax.broadcasted_iota(jnp.int32, (1, LANES), 1) // HEAD_DIM
    r_i = lax.broadcasted_iota(jnp.int32, (tq, tq), 0)
    c_i = lax.broadcasted_iota(jnp.int32, (tq, tq), 1)
    upto = jnp.where(r_i <= c_i, 1.0, 0.0).astype(BF16)

    for h in range(N_IDX_HEADS):
        qp = qix_ref[:, (h // 2) * LANES:(h // 2 + 1) * LANES]
        qm_scr[h] = jnp.where(lane_half == h % 2, qp, jnp.zeros_like(qp))

    def score_body(kb, _):
        k0 = pl.multiple_of(kb * tq, tq)
        kk = ktix_ref[:, pl.ds(k0, tq)]
        sc = jnp.zeros((tq, tq), F32)
        for h in range(N_IDX_HEADS):
            rel = jnp.maximum(_dot(qm_scr[h], kk), 0.0)
            sc = sc + rel * wix_ref[:, h:h + 1]
        causal = (c_i + k0) <= (r_i + qi * tq)
        sc = jnp.where(causal, sc, -jnp.inf)
        key_scr[:, pl.ds(k0, tq)] = _sort_key(sc)
        return 0

    lax.fori_loop(0, nkb, score_body, 0)

    def count_ge(cand):
        def body(kb, part):
            k0 = pl.multiple_of(kb * tq, tq)
            ge = jnp.where(key_scr[:, pl.ds(k0, tq)] >= cand, 1, 0)
            for j in range(tq // LANES):
                part = part + ge[:, j * LANES:(j + 1) * LANES]
            return part
        part = lax.fori_loop(0, nkb, body, jnp.zeros((tq, LANES), jnp.int32))
        return jnp.sum(part, axis=1, keepdims=True)

    def bit_body(it, cur):
        cand = cur + (jnp.int32(1) << (31 - it))
        return jnp.where(count_ge(cand) >= k_sel, cand, cur)

    tau = lax.fori_loop(0, 32, bit_body, jnp.full((tq, 1), -2 ** 31, jnp.int32))
    n_gt = count_ge(tau + 1)
    need = (k_sel - n_gt).astype(F32)

    def bias_body(kb, seen):
        k0 = pl.multiple_of(kb * tq, tq)
        key = key_scr[:, pl.ds(k0, tq)]
        tie = key == tau
        rank = _dot(jnp.where(tie, 1.0, 0.0).astype(BF16), upto) + seen
        sel = (key > tau) | (tie & (rank <= need))
        causal = (c_i + k0) <= (r_i + qi * tq)
        bias_scr[:, pl.ds(k0, tq)] = jnp.where(sel & causal, 0.0, NEG_BIG)
        return rank[:, tq - 1:tq]

    lax.fori_loop(0, nkb, bias_body, jnp.zeros((tq, 1), F32))

    gsz = N_HEADS_SA // N_KV_SA
    ones = jnp.ones((tq, LANES), BF16)
    for g in range(N_KV_SA):
        for hh in range(gsz):
            h = g * gsz + hh
            qp = q_ref[:, (h // 2) * LANES:(h // 2 + 1) * LANES]
            qs_scr[hh * tq:(hh + 1) * tq, :] = jnp.where(lane_half == h % 2, qp, jnp.zeros_like(qp))
        m_scr[...] = jnp.full(m_scr.shape, -jnp.inf, F32)
        acc_scr[...] = jnp.zeros(acc_scr.shape, F32)

        def masked_logits(kb, g=g):
            k0 = pl.multiple_of(kb * tq, tq)
            kk = kt_ref[g * LANES:(g + 1) * LANES, pl.ds(k0, tq)]
            logits = _dot(qs_scr[...], kk)
            return (logits.reshape(gsz, tq, tq) + bias_scr[:, pl.ds(k0, tq)][None]).reshape(gsz * tq, tq)

        def max_body(kb, _):
            logits = masked_logits(kb)
            part = m_scr[...]
            for j in range(tq // LANES):
                part = jnp.maximum(part, logits[:, j * LANES:(j + 1) * LANES])
            m_scr[...] = part
            return 0

        lax.fori_loop(0, nkb, max_body, 0)
        m_scr[...] = jnp.broadcast_to(jnp.max(m_scr[...], axis=1, keepdims=True), m_scr.shape)

        def attn_body(kb, _, g=g):
            k0 = pl.multiple_of(kb * tq, tq)
            logits = masked_logits(kb)
            mrow = m_scr[...]
            p = jnp.concatenate([jnp.exp(logits[:, j * LANES:(j + 1) * LANES] - mrow)
                                 for j in range(tq // LANES)], axis=1).astype(BF16)
            vv = jnp.concatenate([v_ref[pl.ds(k0, tq), g * LANES:(g + 1) * LANES], ones], axis=1)
            acc_scr[...] += _dot(p, vv)
            return 0

        lax.fori_loop(0, nkb, attn_body, 0)
        for hp in range(gsz // 2):
            pair = []
            for hh in (2 * hp, 2 * hp + 1):
                a = acc_scr[hh * tq:(hh + 1) * tq, :]
                pair.append(a[:, :LANES] / a[:, LANES:LANES + 1])
            col = (g * gsz // 2 + hp) * LANES
            o_ref[:, col:col + LANES] = jnp.where(lane_half == 0, pair[0], pair[1]).astype(o_ref.dtype)


def _dsa_attn(q_ix, kt_ix, w_ix, q, kt, v, bsz, seq, tq):
    n, w = q.shape
    nq = seq // tq
    k_sel = min(TOPK_MAX, seq // 4)
    gsz = N_HEADS_SA // N_KV_SA
    rowb = lambda width: pl.BlockSpec((tq, width), lambda b, i: (b * nq + i, 0))
    return pl.pallas_call(
        functools.partial(_dsa_kernel, tq=tq, k_sel=k_sel),
        grid=(bsz, nq),
        in_specs=[
            rowb(q_ix.shape[1]),
            pl.BlockSpec((None, kt_ix.shape[1], seq), lambda b, i: (b, 0, 0)),
            rowb(LANES),
            rowb(w),
            pl.BlockSpec((None, kt.shape[1], seq), lambda b, i: (b, 0, 0)),
            pl.BlockSpec((seq, v.shape[1]), lambda b, i: (b, 0)),
        ],
        out_specs=rowb(w),
        out_shape=jax.ShapeDtypeStruct((n, w), BF16),
        scratch_shapes=[
            pltpu.VMEM((N_IDX_HEADS, tq, LANES), BF16),
            pltpu.VMEM((tq, seq), jnp.int32),
            pltpu.VMEM((tq, seq), F32),
            pltpu.VMEM((gsz * tq, LANES), BF16),
            pltpu.VMEM((gsz * tq, LANES), F32),
            pltpu.VMEM((gsz * tq, 2 * LANES), F32),
        ],
        compiler_params=_cparams(("parallel", "arbitrary")),
        name="dsa_attn",
    )(q_ix, kt_ix, w_ix, q, kt, v)


def _put_row(stack, r, row):
    idx = lax.broadcasted_iota(jnp.int32, stack.shape, 0)
    return jnp.where(idx == r, row, stack)


def _top_desc(x, k):
    out = jnp.zeros((k, x.shape[1]), F32)
    for r in range(k):
        mx = jnp.max(x, axis=0, keepdims=True)
        out = _put_row(out, r, mx)
        x = jnp.where(x == mx, -jnp.inf, x)
    return out


def _merge_kernel(osb_ref, osa_ref, sg_ref, x_ref, gate1_ref, shift2_ref, scale2_ref,
                  gpost_ref, gpre_ref, wup_sb, wup_sa, wout, wqt, keys_ref,
                  x1_o, h2t_o, s_o, st_o, v2_o):
    d = x_ref.shape[1]
    y_sb = _dot(osb_ref[...], wup_sb[...])
    y_sa = _dot(osa_ref[...], wup_sa[...])
    sg = sg_ref[...].astype(F32)
    merged = sg[:, :d] * y_sb + sg[:, d:] * y_sa
    y = _dot(merged.astype(BF16), wout[...])
    x1 = x_ref[...] + gate1_ref[...] * _rms(y, gpost_ref[...])
    x1_o[...] = x1
    h2 = _rms(x1, gpre_ref[...]) * (1.0 + scale2_ref[...]) + shift2_ref[...]
    h2t = h2.T.astype(BF16)
    h2t_o[...] = h2t
    qt = _dot(wqt[...], h2t)

    half = PEER_QDIM // 2
    stats = jnp.zeros(st_o.shape, F32)
    for h in range(PEER_HEADS):
        tops = []
        for p in (0, 1):
            r0 = (2 * h + p) * half
            q_hi, q_lo = _split(qt[r0:r0 + half])
            kf = keys_ref[2 * h + p]
            k_hi, k_lo = _split(kf)
            s = _dot(k_hi, q_hi) + (_dot(k_hi, q_lo) + _dot(k_lo, q_hi))
            s_o[2 * h + p] = s
            tops.append(_top_desc(s, PEER_TOPK))
        v2_o[h * PEER_TOPK:(h + 1) * PEER_TOPK, :] = tops[1]
        hk = PEER_TOPK // 2
        cand = jnp.concatenate(
            [tops[0][0:1] + tops[1]]
            + [tops[0][k1:k1 + 1] + tops[1][0:hk] for k1 in range(1, hk)]
            + [tops[0][hk:] + tops[1][0:1]], axis=0)
        best = _top_desc(cand, PEER_TOPK)
        mx = best[0:1]
        z = jnp.sum(jnp.exp(best - mx), axis=0, keepdims=True)
        stats = _put_row(stats, h, best[PEER_TOPK - 1:PEER_TOPK])
        stats = _put_row(stats, PEER_HEADS + h, mx)
        stats = _put_row(stats, 2 * PEER_HEADS + h, 1.0 / z)
    st_o[...] = stats


def _merge(o_sb, o_sa, sg, x2, ada3, g_post, g_pre, w_up_sb, w_up_sa, w_out, w_query, sub_keys,
           seq, tm):
    n, d = x2.shape
    nt = seq // tm
    nk = 2 * PEER_HEADS
    wqt = w_query.T.astype(BF16)
    keys = sub_keys.reshape(nk, PEER_NKEYS, PEER_QDIM // 2)
    wup_sb, wup_sa, wout = w_up_sb.astype(BF16), w_up_sa.astype(BF16), w_out.astype(BF16)
    row = lambda w: pl.BlockSpec((tm, w), lambda i: (i, 0))
    const = lambda a: pl.BlockSpec(a.shape, lambda i: (0,) * a.ndim)
    adab = lambda j: pl.BlockSpec((None, 1, d), lambda i: (i // nt, 0, j))
    return pl.pallas_call(
        _merge_kernel,
        grid=(n // tm,),
        in_specs=[row(o_sb.shape[1]), row(o_sa.shape[1]), row(2 * d), row(d),
                  adab(2), adab(3), adab(4), const(g_post), const(g_pre),
                  const(wup_sb), const(wup_sa), const(wout), const(wqt), const(keys)],
        out_specs=[row(d),
                   pl.BlockSpec((d, tm), lambda i: (0, i)),
                   pl.BlockSpec((nk, PEER_NKEYS, tm), lambda i: (0, 0, i)),
                   pl.BlockSpec((4 * PEER_HEADS, tm), lambda i: (0, i)),
                   pl.BlockSpec((PEER_HEADS * PEER_TOPK, tm), lambda i: (0, i))],
        out_shape=[jax.ShapeDtypeStruct((n, d), F32),
                   jax.ShapeDtypeStruct((d, n), BF16),
                   jax.ShapeDtypeStruct((nk, PEER_NKEYS, n), F32),
                   jax.ShapeDtypeStruct((4 * PEER_HEADS, n), F32),
                   jax.ShapeDtypeStruct((PEER_HEADS * PEER_TOPK, n), F32)],
        compiler_params=_cparams(("parallel",)),
        name="merge",
    )(o_sb, o_sa, sg, x2, ada3, ada3, ada3, g_post, g_pre, wup_sb, wup_sa, wout, wqt, keys)


_ERFC_P = 0.3275911
_ERFC_C = (0.254829592, -0.284496736, 1.421413741, -1.453152027, 1.061405429)


def _gelu(a):
    u = jnp.abs(a)
    t = 1.0 / (1.0 + (_ERFC_P * 2.0 ** -0.5) * u)
    c = [0.5 * v for v in _ERFC_C]
    q = t * (c[0] + t * (c[1] + t * (c[2] + t * (c[3] + t * c[4])))) * jnp.exp(-0.5 * (a * a))
    return a * jnp.where(a >= 0.0, 1.0 - q, q)


def _peer_kernel(h2t_ref, win_ref, woutt_ref, s_ref, st_ref, v2_ref, y_o,
                 acc_scr, act_scr, th_scr, e1_scr, e2_scr, *, te):
    k = pl.program_id(1)
    nk = PEER_NKEYS

    @pl.when(k == 0)
    def _():
        acc_scr[...] = jnp.zeros(acc_scr.shape, F32)
        for h in range(PEER_HEADS):
            s1 = s_ref[2 * h]
            s2 = s_ref[2 * h + 1]
            tau = st_ref[h:h + 1, :]
            inv_z = st_ref[2 * PEER_HEADS + h:2 * PEER_HEADS + h + 1, :]
            theta = jnp.full(s1.shape, jnp.inf, F32)
            for k2 in range(PEER_TOPK):
                v = v2_ref[h * PEER_TOPK + k2:h * PEER_TOPK + k2 + 1, :]
                theta = jnp.where(s1 + v >= tau, v, theta)
            th_scr[h] = theta
            e1_scr[h] = jnp.exp(s1 - jnp.max(s1, axis=0, keepdims=True))
            e2_scr[h] = jnp.exp(s2 - v2_ref[h * PEER_TOPK:h * PEER_TOPK + 1, :]) * inv_z

    a = _dot(win_ref[...], h2t_ref[...])
    gel = _gelu(a)
    for ii in range(te // nk):
        i = k * (te // nk) + ii
        gate = jnp.zeros((nk, a.shape[1]), F32)
        for h in range(PEER_HEADS):
            sel = s_ref[2 * h + 1] >= th_scr[h, pl.ds(i, 1), :]
            gate = gate + jnp.where(sel, e2_scr[h], 0.0) * e1_scr[h, pl.ds(i, 1), :]
        act_scr[ii * nk:(ii + 1) * nk, :] = (gate * gel[ii * nk:(ii + 1) * nk]).astype(BF16)
    acc_scr[...] += _dot(woutt_ref[...], act_scr[...])

    @pl.when(k == pl.num_programs(1) - 1)
    def _():
        y_o[...] = acc_scr[...].T


def _peer(h2t, w_in_e, w_out_e, s_all, stats, v2top, tm, te):
    d, n = h2t.shape
    n_exp = w_in_e.shape[0]
    win = w_in_e.astype(BF16)
    woutt = w_out_e.T.astype(BF16)
    once = dict(pipeline_mode=pl.Buffered(1))
    tab = pltpu.VMEM((PEER_HEADS, PEER_NKEYS, tm), F32)
    return pl.pallas_call(
        functools.partial(_peer_kernel, te=te),
        grid=(n // tm, n_exp // te),
        in_specs=[
            pl.BlockSpec((d, tm), lambda i, k: (0, i), **once),
            pl.BlockSpec((te, d), lambda i, k: (k, 0)),
            pl.BlockSpec((d, te), lambda i, k: (0, k)),
            pl.BlockSpec((s_all.shape[0], PEER_NKEYS, tm), lambda i, k: (0, 0, i), **once),
            pl.BlockSpec((stats.shape[0], tm), lambda i, k: (0, i), **once),
            pl.BlockSpec((v2top.shape[0], tm), lambda i, k: (0, i), **once),
        ],
        out_specs=pl.BlockSpec((tm, d), lambda i, k: (i, 0)),
        out_shape=jax.ShapeDtypeStruct((n, d), F32),
        scratch_shapes=[pltpu.VMEM((d, tm), F32), pltpu.VMEM((te, tm), BF16), tab, tab, tab],
        compiler_params=_cparams(("parallel", "arbitrary")),
        name="peer",
    )(h2t, win, woutt, s_all, stats, v2top)


def _final_kernel(x1_ref, y_ref, gate_ref, g_ref, o_ref):
    o_ref[...] = x1_ref[...] + gate_ref[...] * _rms(y_ref[...], g_ref[...])


def _final(x1, y, ada3, g_post, seq, tm):
    n, d = x1.shape
    nt = seq // tm
    row = pl.BlockSpec((tm, d), lambda i: (i, 0))
    return pl.pallas_call(
        _final_kernel,
        grid=(n // tm,),
        in_specs=[row, row, pl.BlockSpec((None, 1, d), lambda i: (i // nt, 0, 5)),
                  pl.BlockSpec((1, d), lambda i: (0, 0))],
        out_specs=row,
        out_shape=jax.ShapeDtypeStruct((n, d), F32),
        compiler_params=_cparams(("parallel",)),
        name="final",
    )(x1, y, ada3, g_post)


def _layer(x2, c, pos, bsz, seq, w_ada, b_ada, g_pre_mix, g_post_mix, w_in, w_up_sb, w_up_sa, w_out,
           g_pre_ffn, g_post_ffn, w_peer_query, peer_sub_keys, peer_expert_in, peer_expert_out):
    n, d = x2.shape
    tq = 256 if seq % 256 == 0 else 128
    tm = 256
    row1 = lambda g: g.reshape(1, d)
    ada3 = _ada(c, w_ada, b_ada).reshape(bsz, 1, 6 * d)
    (q_sb, kt_sb, v_sb, q_sa, kt_sa, v_sa, q_ix, kt_ix, w_ix, sg) = _inproj(
        x2, pos, ada3, row1(g_pre_mix), w_in, bsz, seq, tm)
    o_sb = _sb_attn(q_sb, kt_sb, v_sb, bsz, seq, 512 if seq % 512 == 0 else tq)
    o_sa = _dsa_attn(q_ix, kt_ix, w_ix, q_sa, kt_sa, v_sa, bsz, seq, tq)
    x1, h2t, s_all, stats, v2top = _merge(o_sb, o_sa, sg, x2, ada3, row1(g_post_mix), row1(g_pre_ffn),
                                          w_up_sb, w_up_sa, w_out, w_peer_query, peer_sub_keys, seq, tm)
    y = _peer(h2t, peer_expert_in, peer_expert_out, s_all, stats, v2top, tm=min(1024, n), te=512)
    return _final(x1, y, ada3, row1(g_post_ffn), seq, tm)


def kernel(x, c, positions, w_ada, b_ada, g_pre_mix, g_post_mix, w_in, w_up_sb, w_up_sa, w_out,
           g_pre_ffn, g_post_ffn, w_peer_query, peer_sub_keys, peer_expert_in, peer_expert_out):
    bsz, seq, d = x.shape
    x2 = x.reshape(bsz * seq, d)
    for l in range(w_ada.shape[0]):
        x2 = _layer(x2, c, positions, bsz, seq, w_ada[l], b_ada[l], g_pre_mix[l], g_post_mix[l],
                    w_in[l], w_up_sb[l], w_up_sa[l], w_out[l], g_pre_ffn[l], g_post_ffn[l],
                    w_peer_query[l], peer_sub_keys[l], peer_expert_in[l], peer_expert_out[l])
    return x2.reshape(bsz, seq, d)
```

```python
import functools
import math

import numpy as np
import jax
import jax.numpy as jnp
from jax import lax
from jax.experimental import pallas as pl
from jax.experimental.pallas import tpu as pltpu

HEAD_DIM = 64
N_HEADS_SB = 8
N_HEADS_SA = 8
N_KV_SA = 2
N_IDX_HEADS = 8
IDX_DIM = 64
TOPK_MAX = 256
ROPE_THETA = 500000.0
ROT_DIM = HEAD_DIM // 4
ROT_HALF = ROT_DIM // 2
PEER_HEADS = 8
PEER_NKEYS = 128
PEER_QDIM = 256
PEER_TOPK = 16
NORM_EPS = 1e-6

LANES = 128
NEG_BIG = -1e30
VMEM_LIMIT = 56 * 1024 * 1024

F32 = jnp.float32
BF16 = jnp.bfloat16


def _cparams(sem):
    return pltpu.CompilerParams(dimension_semantics=sem, vmem_limit_bytes=VMEM_LIMIT)


def _dot(a, b):
    return jnp.dot(a, b, preferred_element_type=F32)


def _dot_nt(a, b):
    return lax.dot_general(a, b, (((1,), (1,)), ((), ())), preferred_element_type=F32)


def _split(a):
    hi = a.astype(BF16)
    lo = (a - hi.astype(F32)).astype(BF16)
    return hi, lo


def _dot3(a, b):
    ah, al = _split(a)
    bh, bl = _split(b)
    return _dot(ah, bh) + (_dot(ah, bl) + _dot(al, bh))


def _rms(x, g):
    return x * lax.rsqrt(jnp.mean(x * x, axis=-1, keepdims=True) + NORM_EPS) * g


def _ada_kernel(c_ref, w_ref, b_ref, o_ref):
    c = c_ref[...]
    s = c / (1.0 + jnp.exp(-c))
    o_ref[...] = _dot3(s, w_ref[...]) + b_ref[...]


def _ada(c, w, b):
    bsz, d = c.shape
    n_out = w.shape[1]
    return pl.pallas_call(
        _ada_kernel,
        grid=(n_out // d,),
        in_specs=[
            pl.BlockSpec((bsz, d), lambda j: (0, 0)),
            pl.BlockSpec((d, d), lambda j: (0, j)),
            pl.BlockSpec((1, d), lambda j: (0, j)),
        ],
        out_specs=pl.BlockSpec((bsz, d), lambda j: (0, j)),
        out_shape=jax.ShapeDtypeStruct((bsz, n_out), F32),
        compiler_params=_cparams(("parallel",)),
        name="ada",
    )(c, w, b.reshape(1, n_out))


def _rope_rows(x, cos_t, sin_t):
    lane = lax.broadcasted_iota(jnp.int32, (1, LANES), 1) % HEAD_DIM
    c = jnp.where(lane < ROT_DIM, cos_t, 1.0)
    s_lo = jnp.where(lane < ROT_HALF, -sin_t, 0.0)
    s_hi = jnp.where((lane >= ROT_HALF) & (lane < ROT_DIM), sin_t, 0.0)
    outs = []
    for j in range(x.shape[1] // LANES):
        xb = x[:, j * LANES:(j + 1) * LANES]
        up = pltpu.roll(xb, LANES - ROT_HALF, 1)
        dn = pltpu.roll(xb, ROT_HALF, 1)
        outs.append(xb * c + up * s_lo + dn * s_hi)
    return jnp.concatenate(outs, axis=1) if len(outs) > 1 else outs[0]


def _rope_cols(x, cos_t, sin_t):
    outs = []
    for j in range(x.shape[0] // HEAD_DIM):
        blk = x[j * HEAD_DIM:(j + 1) * HEAD_DIM]
        x1 = blk[0:ROT_HALF]
        x2 = blk[ROT_HALF:ROT_DIM]
        outs += [x1 * cos_t - x2 * sin_t, x2 * cos_t + x1 * sin_t, blk[ROT_DIM:]]
    return jnp.concatenate(outs, axis=0)


def _inproj_kernel(x_ref, posc_ref, posr_ref, shift_ref, scale_ref, g_ref, invl_ref, invc_ref,
                   wq_sb, wkt_sb, wv_sb, wq_sa, wkt_sa, wv_sa, wq_ix, wkt_ix, w_aux, w_gate,
                   q_sb_o, kt_sb_o, v_sb_o, q_sa_o, kt_sa_o, v_sa_o, q_ix_o, kt_ix_o, wix_o, sg_o):
    x = x_ref[...]
    h = _rms(x, g_ref[...]) * (1.0 + scale_ref[...]) + shift_ref[...]
    hb = h.astype(BF16)

    ang_r = posc_ref[...].astype(F32) * invl_ref[...]
    cos_r, sin_r = jnp.cos(ang_r), jnp.sin(ang_r)
    ang_c = posr_ref[...].astype(F32) * invc_ref[...]
    cos_c, sin_c = jnp.cos(ang_c), jnp.sin(ang_c)

    scale = HEAD_DIM ** -0.5
    q_sb_o[...] = (_dot(hb, wq_sb[...]) * scale).astype(BF16)
    kt_sb_o[...] = _dot_nt(wkt_sb[...], hb).astype(BF16)
    v_sb_o[...] = _dot(hb, wv_sb[...]).astype(BF16)
    q_sa_o[...] = (_rope_rows(_dot(hb, wq_sa[...]), cos_r, sin_r) * scale).astype(BF16)
    kt_sa_o[...] = _rope_cols(_dot_nt(wkt_sa[...], hb), cos_c, sin_c).astype(BF16)
    v_sa_o[...] = _dot(hb, wv_sa[...]).astype(BF16)
    q_ix_o[...] = _rope_rows(_dot(hb, wq_ix[...]), cos_r, sin_r).astype(BF16)
    kt_ix_o[...] = _rope_cols(_dot_nt(wkt_ix[...], hb), cos_c, sin_c).astype(BF16)
    wix_o[...] = _dot(hb, w_aux[...]) * (IDX_DIM ** -0.5 * N_IDX_HEADS ** -0.5)
    gate = _dot(hb, w_gate[...])
    sg_o[...] = (1.0 / (1.0 + jnp.exp(-gate))).astype(BF16)


def _inproj(x2, pos, shift_scale, g_pre, w_in, bsz, seq, tm):
    n, d = x2.shape
    w_sb = N_HEADS_SB * HEAD_DIM
    w_sa = N_HEADS_SA * HEAD_DIM
    w_kv = N_KV_SA * HEAD_DIM
    w_ixq = N_IDX_HEADS * IDX_DIM
    cuts = np.cumsum([w_sb, w_sb, w_sb, w_sa, w_kv, w_kv, w_ixq, IDX_DIM, N_IDX_HEADS, d, d])
    c = [0] + [int(v) for v in cuts]
    wb = w_in.astype(BF16)
    seg = lambda i: wb[:, c[i]:c[i + 1]]
    wq_sb, wk_sb, wv_sb, wq_sa, wk_sa, wv_sa, wq_ix, wk_ix, w_ixw = [seg(i) for i in range(9)]
    w_gate = wb[:, c[9]:c[11]]
    dup = lambda w: jnp.concatenate(
        [w[:, j * HEAD_DIM:(j + 1) * HEAD_DIM] for j in range(w.shape[1] // HEAD_DIM) for _ in (0, 1)], axis=1)
    wkt_sb = wk_sb.T
    wkt_sa = dup(wk_sa).T
    wv_sa2 = dup(wv_sa)
    wkt_ix = dup(wk_ix).T
    w_aux = jnp.concatenate([w_ixw, jnp.zeros((d, LANES - N_IDX_HEADS), BF16)], axis=1)

    inv_freq = jnp.power(ROPE_THETA, -2.0 * jnp.arange(ROT_HALF, dtype=F32) / ROT_DIM)
    inv_lane = jnp.tile(inv_freq, LANES // ROT_HALF).reshape(1, LANES)
    inv_col = jnp.broadcast_to(inv_freq[:, None], (ROT_HALF, tm))

    nt = seq // tm
    row = lambda w: pl.BlockSpec((tm, w), lambda i: (i, 0))
    colT = lambda r: pl.BlockSpec((None, r, tm), lambda i: (i // nt, 0, i % nt))
    const = lambda a: pl.BlockSpec(a.shape, lambda i: (0,) * a.ndim)
    weights = [wq_sb, wkt_sb, wv_sb, wq_sa, wkt_sa, wv_sa2, wq_ix, wkt_ix, w_aux, w_gate]
    out_shape = [
        jax.ShapeDtypeStruct((n, w_sb), BF16),
        jax.ShapeDtypeStruct((bsz, w_sb, seq), BF16),
        jax.ShapeDtypeStruct((n, w_sb), BF16),
        jax.ShapeDtypeStruct((n, w_sa), BF16),
        jax.ShapeDtypeStruct((bsz, 2 * w_kv, seq), BF16),
        jax.ShapeDtypeStruct((n, 2 * w_kv), BF16),
        jax.ShapeDtypeStruct((n, w_ixq), BF16),
        jax.ShapeDtypeStruct((bsz, 2 * IDX_DIM, seq), BF16),
        jax.ShapeDtypeStruct((n, LANES), F32),
        jax.ShapeDtypeStruct((n, 2 * d), BF16),
    ]
    out_specs = [row(w_sb), colT(w_sb), row(w_sb), row(w_sa), colT(2 * w_kv), row(2 * w_kv),
                 row(w_ixq), colT(2 * IDX_DIM), row(LANES), row(2 * d)]
    return pl.pallas_call(
        _inproj_kernel,
        grid=(n // tm,),
        in_specs=[
            row(d),
            pl.BlockSpec((tm, 1), lambda i: (i, 0)),
            pl.BlockSpec((None, 1, tm), lambda i: (i // nt, 0, i % nt)),
            pl.BlockSpec((None, 1, d), lambda i: (i // nt, 0, 0)),
            pl.BlockSpec((None, 1, d), lambda i: (i // nt, 0, 1)),
            const(g_pre), const(inv_lane), const(inv_col),
        ] + [const(w) for w in weights],
        out_specs=out_specs,
        out_shape=out_shape,
        compiler_params=_cparams(("parallel",)),
        name="inproj",
    )(x2, pos.reshape(n, 1), pos.reshape(bsz, 1, seq), shift_scale, shift_scale, g_pre,
      inv_lane, inv_col, *weights)


def _softplus(z):
    return jnp.maximum(z, 0.0) + jnp.log(1.0 + jnp.exp(-jnp.abs(z)))


def _sb_kernel(q_ref, kt_ref, v_ref, o_ref, *, tq):
    qi = pl.program_id(2)
    q2 = q_ref[...]
    lane_half = lax.broadcasted_iota(jnp.int32, (1, LANES), 1) // HEAD_DIM
    r_i = lax.broadcasted_iota(jnp.int32, (tq, tq), 0)
    c_i = lax.broadcasted_iota(jnp.int32, (tq, tq), 1)
    strict = c_i < r_i
    ck = min(tq, 2 * LANES)
    later = jnp.where(lax.broadcasted_iota(jnp.int32, (ck, ck), 0) > lax.broadcasted_iota(jnp.int32, (ck, ck), 1),
                      1.0, 0.0).astype(BF16)
    qms = [jnp.where(lane_half == hh, q2, jnp.zeros_like(q2)) for hh in (0, 1)]

    def block(kb, state, diag):
        k0 = pl.multiple_of(kb * tq, tq)
        kt = kt_ref[:, pl.ds(k0, tq)]
        v = v_ref[pl.ds(k0, tq), :]
        new = []
        for hh in (0, 1):
            carry, acc = state[2 * hh], state[2 * hh + 1]
            z = _dot(qms[hh], kt)
            sp = _softplus(z)
            lf = -sp
            if diag:
                lf = jnp.where(strict, lf, 0.0)
            chunks = [None] * (tq // ck)
            for c in reversed(range(tq // ck)):
                lf_c = lf[:, c * ck:(c + 1) * ck]
                lf_hi, lf_lo = _split(lf_c)
                chunks[c] = _dot(lf_hi, later) + _dot(lf_lo, later) + carry
                carry = carry + jnp.sum(lf_c, axis=1, keepdims=True)
            between = jnp.concatenate(chunks, axis=1) if len(chunks) > 1 else chunks[0]
            a = jnp.exp((z - sp) + between)
            if diag:
                a = jnp.where(strict, a, 0.0)
            new += [carry, acc + _dot(a.astype(BF16), v)]
        return tuple(new)

    zero = (jnp.zeros((tq, 1), F32), jnp.zeros((tq, LANES), F32))
    state = block(qi, zero + zero, True)
    state = lax.fori_loop(0, qi, lambda it, st: block(qi - 1 - it, st, False), state)
    o_ref[...] = jnp.where(lane_half == 0, state[1], state[3]).astype(o_ref.dtype)


def _sb_attn(q, kt, v, bsz, seq, tq):
    n, w = q.shape
    npair = w // LANES
    nq = seq // tq
    return pl.pallas_call(
        functools.partial(_sb_kernel, tq=tq),
        grid=(bsz, npair, nq),
        in_specs=[
            pl.BlockSpec((tq, LANES), lambda b, p, i: (b * nq + i, p)),
            pl.BlockSpec((None, LANES, seq), lambda b, p, i: (b, p, 0)),
            pl.BlockSpec((seq, LANES), lambda b, p, i: (b, p)),
        ],
        out_specs=pl.BlockSpec((tq, LANES), lambda b, p, i: (b * nq + i, p)),
        out_shape=jax.ShapeDtypeStruct((n, w), BF16),
        compiler_params=_cparams(("parallel", "parallel", "arbitrary")),
        name="sb_attn",
    )(q, kt, v)


def _sort_key(s):
    bits = lax.bitcast_convert_type(s, jnp.int32)
    bits = jnp.where(bits == jnp.int32(-2 ** 31), 0, bits)
    return bits ^ ((bits >> 31) & jnp.int32(0x7FFFFFFF))


def _dsa_kernel(qix_ref, ktix_ref, wix_ref, q_ref, kt_ref, v_ref, o_ref,
                qm_scr, key_scr, bias_scr, qs_scr, m_scr, acc_scr, *, tq, k_sel):
    qi = pl.program_id(1)
    nkb = qi + 1
    lane_half = lax.broadcasted_iota(jnp.int32, (1, LANES), 1) // HEAD_DIM
    r_i = lax.broadcasted_iota(jnp.int32, (tq, tq), 0)
    c_i = lax.broadcasted_iota(jnp.int32, (tq, tq), 1)
    upto = jnp.where(r_i <= c_i, 1.0, 0.0).astype(BF16)

    for h in range(N_IDX_HEADS):
        qp = qix_ref[:, (h // 2) * LANES:(h // 2 + 1) * LANES]
        qm_scr[h] = jnp.where(lane_half == h % 2, qp, jnp.zeros_like(qp))

    def score_body(kb, _):
        k0 = pl.multiple_of(kb * tq, tq)
        kk = ktix_ref[:, pl.ds(k0, tq)]
        sc = jnp.zeros((tq, tq), F32)
        for h in range(N_IDX_HEADS):
            rel = jnp.maximum(_dot(qm_scr[h], kk), 0.0)
            sc = sc + rel * wix_ref[:, h:h + 1]
        causal = (c_i + k0) <= (r_i + qi * tq)
        sc = jnp.where(causal, sc, -jnp.inf)
        key_scr[:, pl.ds(k0, tq)] = _sort_key(sc)
        return 0

    lax.fori_loop(0, nkb, score_body, 0)

    def count_ge(cand):
        def body(kb, part):
            k0 = pl.multiple_of(kb * tq, tq)
            ge = jnp.where(key_scr[:, pl.ds(k0, tq)] >= cand, 1, 0)
            for j in range(tq // LANES):
                part = part + ge[:, j * LANES:(j + 1) * LANES]
            return part
        part = lax.fori_loop(0, nkb, body, jnp.zeros((tq, LANES), jnp.int32))
        return jnp.sum(part, axis=1, keepdims=True)

    def bit_body(it, cur):
        cand = cur + (jnp.int32(1) << (31 - it))
        return jnp.where(count_ge(cand) >= k_sel, cand, cur)

    tau = lax.fori_loop(0, 32, bit_body, jnp.full((tq, 1), -2 ** 31, jnp.int32))
    n_gt = count_ge(tau + 1)
    need = (k_sel - n_gt).astype(F32)

    def bias_body(kb, seen):
        k0 = pl.multiple_of(kb * tq, tq)
        key = key_scr[:, pl.ds(k0, tq)]
        tie = key == tau
        rank = _dot(jnp.where(tie, 1.0, 0.0).astype(BF16), upto) + seen
        sel = (key > tau) | (tie & (rank <= need))
        causal = (c_i + k0) <= (r_i + qi * tq)
        bias_scr[:, pl.ds(k0, tq)] = jnp.where(sel & causal, 0.0, NEG_BIG)
        return rank[:, tq - 1:tq]

    lax.fori_loop(0, nkb, bias_body, jnp.zeros((tq, 1), F32))

    gsz = N_HEADS_SA // N_KV_SA
    ones = jnp.ones((tq, LANES), BF16)
    for g in range(N_KV_SA):
        for hh in range(gsz):
            h = g * gsz + hh
            qp = q_ref[:, (h // 2) * LANES:(h // 2 + 1) * LANES]
            qs_scr[hh * tq:(hh + 1) * tq, :] = jnp.where(lane_half == h % 2, qp, jnp.zeros_like(qp))
        m_scr[...] = jnp.full(m_scr.shape, -jnp.inf, F32)
        acc_scr[...] = jnp.zeros(acc_scr.shape, F32)

        def masked_logits(kb, g=g):
            k0 = pl.multiple_of(kb * tq, tq)
            kk = kt_ref[g * LANES:(g + 1) * LANES, pl.ds(k0, tq)]
            logits = _dot(qs_scr[...], kk)
            return (logits.reshape(gsz, tq, tq) + bias_scr[:, pl.ds(k0, tq)][None]).reshape(gsz * tq, tq)

        def max_body(kb, _):
            logits = masked_logits(kb)
            part = m_scr[...]
            for j in range(tq // LANES):
                part = jnp.maximum(part, logits[:, j * LANES:(j + 1) * LANES])
            m_scr[...] = part
            return 0

        lax.fori_loop(0, nkb, max_body, 0)
        m_scr[...] = jnp.broadcast_to(jnp.max(m_scr[...], axis=1, keepdims=True), m_scr.shape)

        def attn_body(kb, _, g=g):
            k0 = pl.multiple_of(kb * tq, tq)
            logits = masked_logits(kb)
            mrow = m_scr[...]
            p = jnp.concatenate([jnp.exp(logits[:, j * LANES:(j + 1) * LANES] - mrow)
                                 for j in range(tq // LANES)], axis=1).astype(BF16)
            vv = jnp.concatenate([v_ref[pl.ds(k0, tq), g * LANES:(g + 1) * LANES], ones], axis=1)
            acc_scr[...] += _dot(p, vv)
            return 0

        lax.fori_loop(0, nkb, attn_body, 0)
        for hp in range(gsz // 2):
            pair = []
            for hh in (2 * hp, 2 * hp + 1):
                a = acc_scr[hh * tq:(hh + 1) * tq, :]
                pair.append(a[:, :LANES] / a[:, LANES:LANES + 1])
            col = (g * gsz // 2 + hp) * LANES
            o_ref[:, col:col + LANES] = jnp.where(lane_half == 0, pair[0], pair[1]).astype(o_ref.dtype)


def _dsa_attn(q_ix, kt_ix, w_ix, q, kt, v, bsz, seq, tq):
    n, w = q.shape
    nq = seq // tq
    k_sel = min(TOPK_MAX, seq // 4)
    gsz = N_HEADS_SA // N_KV_SA
    rowb = lambda width: pl.BlockSpec((tq, width), lambda b, i: (b * nq + i, 0))
    return pl.pallas_call(
        functools.partial(_dsa_kernel, tq=tq, k_sel=k_sel),
        grid=(bsz, nq),
        in_specs=[
            rowb(q_ix.shape[1]),
            pl.BlockSpec((None, kt_ix.shape[1], seq), lambda b, i: (b, 0, 0)),
            rowb(LANES),
            rowb(w),
            pl.BlockSpec((None, kt.shape[1], seq), lambda b, i: (b, 0, 0)),
            pl.BlockSpec((seq, v.shape[1]), lambda b, i: (b, 0)),
        ],
        out_specs=rowb(w),
        out_shape=jax.ShapeDtypeStruct((n, w), BF16),
        scratch_shapes=[
            pltpu.VMEM((N_IDX_HEADS, tq, LANES), BF16),
            pltpu.VMEM((tq, seq), jnp.int32),
            pltpu.VMEM((tq, seq), F32),
            pltpu.VMEM((gsz * tq, LANES), BF16),
            pltpu.VMEM((gsz * tq, LANES), F32),
            pltpu.VMEM((gsz * tq, 2 * LANES), F32),
        ],
        compiler_params=_cparams(("parallel", "arbitrary")),
        name="dsa_attn",
    )(q_ix, kt_ix, w_ix, q, kt, v)


def _put_row(stack, r, row):
    idx = lax.broadcasted_iota(jnp.int32, stack.shape, 0)
    return jnp.where(idx == r, row, stack)


def _top_desc(x, k):
    out = jnp.zeros((k, x.shape[1]), F32)
    for r in range(k):
        mx = jnp.max(x, axis=0, keepdims=True)
        out = _put_row(out, r, mx)
        x = jnp.where(x == mx, -jnp.inf, x)
    return out


def _merge_kernel(osb_ref, osa_ref, sg_ref, x_ref, gate1_ref, shift2_ref, scale2_ref,
                  gpost_ref, gpre_ref, wup_sb, wup_sa, wout, wqt, keys_ref,
                  x1_o, h2t_o, s_o, st_o, v2_o):
    d = x_ref.shape[1]
    y_sb = _dot(osb_ref[...], wup_sb[...])
    y_sa = _dot(osa_ref[...], wup_sa[...])
    sg = sg_ref[...].astype(F32)
    merged = sg[:, :d] * y_sb + sg[:, d:] * y_sa
    y = _dot(merged.astype(BF16), wout[...])
    x1 = x_ref[...] + gate1_ref[...] * _rms(y, gpost_ref[...])
    x1_o[...] = x1
    h2 = _rms(x1, gpre_ref[...]) * (1.0 + scale2_ref[...]) + shift2_ref[...]
    h2t = h2.T.astype(BF16)
    h2t_o[...] = h2t
    qt = _dot(wqt[...], h2t)

    half = PEER_QDIM // 2
    stats = jnp.zeros(st_o.shape, F32)
    for h in range(PEER_HEADS):
        tops = []
        for p in (0, 1):
            r0 = (2 * h + p) * half
            q_hi, q_lo = _split(qt[r0:r0 + half])
            kf = keys_ref[2 * h + p]
            k_hi, k_lo = _split(kf)
            s = _dot(k_hi, q_hi) + (_dot(k_hi, q_lo) + _dot(k_lo, q_hi))
            s_o[2 * h + p] = s
            tops.append(_top_desc(s, PEER_TOPK))
        v2_o[h * PEER_TOPK:(h + 1) * PEER_TOPK, :] = tops[1]
        hk = PEER_TOPK // 2
        cand = jnp.concatenate(
            [tops[0][0:1] + tops[1]]
            + [tops[0][k1:k1 + 1] + tops[1][0:hk] for k1 in range(1, hk)]
            + [tops[0][hk:] + tops[1][0:1]], axis=0)
        best = _top_desc(cand, PEER_TOPK)
        mx = best[0:1]
        z = jnp.sum(jnp.exp(best - mx), axis=0, keepdims=True)
        stats = _put_row(stats, h, best[PEER_TOPK - 1:PEER_TOPK])
        stats = _put_row(stats, PEER_HEADS + h, mx)
        stats = _put_row(stats, 2 * PEER_HEADS + h, 1.0 / z)
    st_o[...] = stats


def _merge(o_sb, o_sa, sg, x2, ada3, g_post, g_pre, w_up_sb, w_up_sa, w_out, w_query, sub_keys,
           seq, tm):
    n, d = x2.shape
    nt = seq // tm
    nk = 2 * PEER_HEADS
    wqt = w_query.T.astype(BF16)
    keys = sub_keys.reshape(nk, PEER_NKEYS, PEER_QDIM // 2)
    wup_sb, wup_sa, wout = w_up_sb.astype(BF16), w_up_sa.astype(BF16), w_out.astype(BF16)
    row = lambda w: pl.BlockSpec((tm, w), lambda i: (i, 0))
    const = lambda a: pl.BlockSpec(a.shape, lambda i: (0,) * a.ndim)
    adab = lambda j: pl.BlockSpec((None, 1, d), lambda i: (i // nt, 0, j))
    return pl.pallas_call(
        _merge_kernel,
        grid=(n // tm,),
        in_specs=[row(o_sb.shape[1]), row(o_sa.shape[1]), row(2 * d), row(d),
                  adab(2), adab(3), adab(4), const(g_post), const(g_pre),
                  const(wup_sb), const(wup_sa), const(wout), const(wqt), const(keys)],
        out_specs=[row(d),
                   pl.BlockSpec((d, tm), lambda i: (0, i)),
                   pl.BlockSpec((nk, PEER_NKEYS, tm), lambda i: (0, 0, i)),
                   pl.BlockSpec((4 * PEER_HEADS, tm), lambda i: (0, i)),
                   pl.BlockSpec((PEER_HEADS * PEER_TOPK, tm), lambda i: (0, i))],
        out_shape=[jax.ShapeDtypeStruct((n, d), F32),
                   jax.ShapeDtypeStruct((d, n), BF16),
                   jax.ShapeDtypeStruct((nk, PEER_NKEYS, n), F32),
                   jax.ShapeDtypeStruct((4 * PEER_HEADS, n), F32),
                   jax.ShapeDtypeStruct((PEER_HEADS * PEER_TOPK, n), F32)],
        compiler_params=_cparams(("parallel",)),
        name="merge",
    )(o_sb, o_sa, sg, x2, ada3, ada3, ada3, g_post, g_pre, wup_sb, wup_sa, wout, wqt, keys)


_ERFC_P = 0.3275911
_ERFC_C = (0.254829592, -0.284496736, 1.421413741, -1.453152027, 1.061405429)


def _gelu(a):
    u = jnp.abs(a)
    t = 1.0 / (1.0 + (_ERFC_P * 2.0 ** -0.5) * u)
    c = [0.5 * v for v in _ERFC_C]
    q = t * (c[0] + t * (c[1] + t * (c[2] + t * (c[3] + t * c[4])))) * jnp.exp(-0.5 * (a * a))
    return a * jnp.where(a >= 0.0, 1.0 - q, q)


BF16_ROWS = 16


def _pair_bits(x):
    u = lax.bitcast_convert_type(x.astype(BF16).astype(F32), jnp.uint32)
    return u | (u >> 16)


def _bcast_bf16(row_u32):
    tm = row_u32.shape[1]
    return pltpu.bitcast(jnp.broadcast_to(row_u32, (BF16_ROWS // 2, tm)), BF16)


def _peer_kernel(h2t_ref, win_ref, woutt_ref, s_ref, st_ref, v2_ref, x1_ref, gate2_ref, gpost_ref, o_ref,
                 acc_scr, act_scr, cnt_scr, e1_scr, rk_scr, e2_scr, *, te):
    k = pl.program_id(1)
    nk = PEER_NKEYS
    tm = h2t_ref.shape[1]

    @pl.when(k == 0)
    def _():
        acc_scr[...] = jnp.zeros(acc_scr.shape, F32)
        for h in range(PEER_HEADS):
            s1 = s_ref[2 * h]
            s2 = s_ref[2 * h + 1]
            tau = st_ref[h:h + 1, :]
            inv_z = st_ref[2 * PEER_HEADS + h:2 * PEER_HEADS + h + 1, :]
            cnt = jnp.zeros(s1.shape, F32)
            rank = jnp.zeros(s2.shape, F32)
            for k2 in range(PEER_TOPK):
                v = v2_ref[h * PEER_TOPK + k2:h * PEER_TOPK + k2 + 1, :]
                cnt = jnp.where(s1 + v >= tau, k2 + 1.0, cnt)
                rank = jnp.where(v > s2, k2 + 1.0, rank)
            cnt_scr[h] = _pair_bits(cnt)
            e1_scr[h] = _pair_bits(jnp.exp(s1 - jnp.max(s1, axis=0, keepdims=True)))
            rk_scr[h] = rank.astype(BF16)
            e2_scr[h] = (jnp.exp(s2 - v2_ref[h * PEER_TOPK:h * PEER_TOPK + 1, :]) * inv_z).astype(BF16)

    a = _dot(win_ref[...], h2t_ref[...])
    gel = _gelu(a)
    grp = nk // BF16_ROWS
    for ii in range(te // nk):
        i = k * (te // nk) + ii
        gate = jnp.zeros((grp, BF16_ROWS, tm), BF16)
        for h in range(PEER_HEADS):
            cnt = _bcast_bf16(cnt_scr[h, pl.ds(i, 1), :])[None]
            e1 = _bcast_bf16(e1_scr[h, pl.ds(i, 1), :])[None]
            rk = rk_scr[h].reshape(grp, BF16_ROWS, tm)
            e2 = e2_scr[h].reshape(grp, BF16_ROWS, tm)
            gate = gate + jnp.where(rk < cnt, e2, jnp.zeros_like(e2)) * e1
        act_scr[ii * nk:(ii + 1) * nk, :] = gate.reshape(nk, tm) * gel[ii * nk:(ii + 1) * nk].astype(BF16)
    acc_scr[...] += _dot(woutt_ref[...], act_scr[...])

    @pl.when(k == pl.num_programs(1) - 1)
    def _():
        o_ref[...] = x1_ref[...] + gate2_ref[...] * _rms(acc_scr[...].T, gpost_ref[...])


def _peer(h2t, w_in_e, w_out_e, s_all, stats, v2top, x1, ada3, g_post, seq, tm, te):
    d, n = h2t.shape
    nt = seq // tm
    n_exp = w_in_e.shape[0]
    win = w_in_e.astype(BF16)
    woutt = w_out_e.T.astype(BF16)
    once = dict(pipeline_mode=pl.Buffered(1))
    tab_u = pltpu.VMEM((PEER_HEADS, PEER_NKEYS, tm), jnp.uint32)
    tab_b = pltpu.VMEM((PEER_HEADS, PEER_NKEYS, tm), BF16)
    return pl.pallas_call(
        functools.partial(_peer_kernel, te=te),
        grid=(n // tm, n_exp // te),
        in_specs=[
            pl.BlockSpec((d, tm), lambda i, k: (0, i), **once),
            pl.BlockSpec((te, d), lambda i, k: (k, 0)),
            pl.BlockSpec((d, te), lambda i, k: (0, k)),
            pl.BlockSpec((s_all.shape[0], PEER_NKEYS, tm), lambda i, k: (0, 0, i), **once),
            pl.BlockSpec((stats.shape[0], tm), lambda i, k: (0, i), **once),
            pl.BlockSpec((v2top.shape[0], tm), lambda i, k: (0, i), **once),
            pl.BlockSpec((tm, d), lambda i, k: (i, 0), **once),
            pl.BlockSpec((None, 1, d), lambda i, k: (i // nt, 0, 5)),
            pl.BlockSpec((1, d), lambda i, k: (0, 0)),
        ],
        out_specs=pl.BlockSpec((tm, d), lambda i, k: (i, 0)),
        out_shape=jax.ShapeDtypeStruct((n, d), F32),
        scratch_shapes=[pltpu.VMEM((d, tm), F32), pltpu.VMEM((te, tm), BF16), tab_u, tab_u, tab_b, tab_b],
        compiler_params=_cparams(("parallel", "arbitrary")),
        name="peer",
    )(h2t, win, woutt, s_all, stats, v2top, x1, ada3, g_post)


def _layer(x2, c, pos, bsz, seq, w_ada, b_ada, g_pre_mix, g_post_mix, w_in, w_up_sb, w_up_sa, w_out,
           g_pre_ffn, g_post_ffn, w_peer_query, peer_sub_keys, peer_expert_in, peer_expert_out):
    n, d = x2.shape
    tq = 256 if seq % 256 == 0 else 128
    tm = 256
    row1 = lambda g: g.reshape(1, d)
    ada3 = _ada(c, w_ada, b_ada).reshape(bsz, 1, 6 * d)
    (q_sb, kt_sb, v_sb, q_sa, kt_sa, v_sa, q_ix, kt_ix, w_ix, sg) = _inproj(
        x2, pos, ada3, row1(g_pre_mix), w_in, bsz, seq, tm)
    o_sb = _sb_attn(q_sb, kt_sb, v_sb, bsz, seq, 512 if seq % 512 == 0 else tq)
    o_sa = _dsa_attn(q_ix, kt_ix, w_ix, q_sa, kt_sa, v_sa, bsz, seq, tq)
    x1, h2t, s_all, stats, v2top = _merge(o_sb, o_sa, sg, x2, ada3, row1(g_post_mix), row1(g_pre_ffn),
                                          w_up_sb, w_up_sa, w_out, w_peer_query, peer_sub_keys, seq, tm)
    return _peer(h2t, peer_expert_in, peer_expert_out, s_all, stats, v2top, x1, ada3, row1(g_post_ffn),
                 seq, tm=min(1024, seq), te=512)


def kernel(x, c, positions, w_ada, b_ada, g_pre_mix, g_post_mix, w_in, w_up_sb, w_up_sa, w_out,
           g_pre_ffn, g_post_ffn, w_peer_query, peer_sub_keys, peer_expert_in, peer_expert_out):
    bsz, seq, d = x.shape
    x2 = x.reshape(bsz * seq, d)
    for l in range(w_ada.shape[0]):
        x2 = _layer(x2, c, positions, bsz, seq, w_ada[l], b_ada[l], g_pre_mix[l], g_post_mix[l],
                    w_in[l], w_up_sb[l], w_up_sa[l], w_out[l], g_pre_ffn[l], g_post_ffn[l],
                    w_peer_query[l], peer_sub_keys[l], peer_expert_in[l], peer_expert_out[l])
    return x2.reshape(bsz, seq, d)
```

```python
import functools
import math

import numpy as np
import jax
import jax.numpy as jnp
from jax import lax
from jax.experimental import pallas as pl
from jax.experimental.pallas import tpu as pltpu

HEAD_DIM = 64
N_HEADS_SB = 8
N_HEADS_SA = 8
N_KV_SA = 2
N_IDX_HEADS = 8
IDX_DIM = 64
TOPK_MAX = 256
ROPE_THETA = 500000.0
ROT_DIM = HEAD_DIM // 4
ROT_HALF = ROT_DIM // 2
PEER_HEADS = 8
PEER_NKEYS = 128
PEER_QDIM = 256
PEER_TOPK = 16
NORM_EPS = 1e-6

LANES = 128
NEG_BIG = -1e30
VMEM_LIMIT = 56 * 1024 * 1024

F32 = jnp.float32
BF16 = jnp.bfloat16


def _cparams(sem):
    return pltpu.CompilerParams(dimension_semantics=sem, vmem_limit_bytes=VMEM_LIMIT)


def _dot(a, b):
    return jnp.dot(a, b, preferred_element_type=F32)


def _dot_nt(a, b):
    return lax.dot_general(a, b, (((1,), (1,)), ((), ())), preferred_element_type=F32)


def _split(a):
    hi = a.astype(BF16)
    lo = (a - hi.astype(F32)).astype(BF16)
    return hi, lo


def _dot3(a, b):
    ah, al = _split(a)
    bh, bl = _split(b)
    return _dot(ah, bh) + (_dot(ah, bl) + _dot(al, bh))


def _rms(x, g):
    return x * lax.rsqrt(jnp.mean(x * x, axis=-1, keepdims=True) + NORM_EPS) * g


def _ada_kernel(c_ref, w_ref, b_ref, o_ref):
    c = c_ref[...]
    s = c / (1.0 + jnp.exp(-c))
    o_ref[...] = _dot3(s, w_ref[...]) + b_ref[...]


def _ada(c, w, b):
    bsz, d = c.shape
    n_out = w.shape[1]
    return pl.pallas_call(
        _ada_kernel,
        grid=(n_out // d,),
        in_specs=[
            pl.BlockSpec((bsz, d), lambda j: (0, 0)),
            pl.BlockSpec((d, d), lambda j: (0, j)),
            pl.BlockSpec((1, d), lambda j: (0, j)),
        ],
        out_specs=pl.BlockSpec((bsz, d), lambda j: (0, j)),
        out_shape=jax.ShapeDtypeStruct((bsz, n_out), F32),
        compiler_params=_cparams(("parallel",)),
        name="ada",
    )(c, w, b.reshape(1, n_out))


def _rope_rows(x, cos_t, sin_t):
    lane = lax.broadcasted_iota(jnp.int32, (1, LANES), 1) % HEAD_DIM
    c = jnp.where(lane < ROT_DIM, cos_t, 1.0)
    s_lo = jnp.where(lane < ROT_HALF, -sin_t, 0.0)
    s_hi = jnp.where((lane >= ROT_HALF) & (lane < ROT_DIM), sin_t, 0.0)
    outs = []
    for j in range(x.shape[1] // LANES):
        xb = x[:, j * LANES:(j + 1) * LANES]
        up = pltpu.roll(xb, LANES - ROT_HALF, 1)
        dn = pltpu.roll(xb, ROT_HALF, 1)
        outs.append(xb * c + up * s_lo + dn * s_hi)
    return jnp.concatenate(outs, axis=1) if len(outs) > 1 else outs[0]


def _rope_cols(x, cos_t, sin_t):
    outs = []
    for j in range(x.shape[0] // HEAD_DIM):
        blk = x[j * HEAD_DIM:(j + 1) * HEAD_DIM]
        x1 = blk[0:ROT_HALF]
        x2 = blk[ROT_HALF:ROT_DIM]
        outs += [x1 * cos_t - x2 * sin_t, x2 * cos_t + x1 * sin_t, blk[ROT_DIM:]]
    return jnp.concatenate(outs, axis=0)


def _inproj_kernel(x_ref, posc_ref, posr_ref, shift_ref, scale_ref, g_ref, invl_ref, invc_ref,
                   wq_sb, wkt_sb, wv_sb, wq_sa, wkt_sa, wv_sa, wq_ix, wkt_ix, w_aux, w_gate,
                   q_sb_o, kt_sb_o, v_sb_o, q_sa_o, kt_sa_o, v_sa_o, q_ix_o, kt_ix_o, wix_o, sg_o):
    x = x_ref[...]
    h = _rms(x, g_ref[...]) * (1.0 + scale_ref[...]) + shift_ref[...]
    hb = h.astype(BF16)

    ang_r = posc_ref[...].astype(F32) * invl_ref[...]
    cos_r, sin_r = jnp.cos(ang_r), jnp.sin(ang_r)
    ang_c = posr_ref[...].astype(F32) * invc_ref[...]
    cos_c, sin_c = jnp.cos(ang_c), jnp.sin(ang_c)

    scale = HEAD_DIM ** -0.5
    q_sb_o[...] = (_dot(hb, wq_sb[...]) * scale).astype(BF16)
    kt_sb_o[...] = _dot_nt(wkt_sb[...], hb).astype(BF16)
    v_sb_o[...] = _dot(hb, wv_sb[...]).astype(BF16)
    q_sa_o[...] = (_rope_rows(_dot(hb, wq_sa[...]), cos_r, sin_r) * scale).astype(BF16)
    kt_sa_o[...] = _rope_cols(_dot_nt(wkt_sa[...], hb), cos_c, sin_c).astype(BF16)
    v_sa_o[...] = _dot(hb, wv_sa[...]).astype(BF16)
    q_ix_o[...] = _rope_rows(_dot(hb, wq_ix[...]), cos_r, sin_r).astype(BF16)
    kt_ix_o[...] = _rope_cols(_dot_nt(wkt_ix[...], hb), cos_c, sin_c).astype(BF16)
    wix_o[...] = _dot(hb, w_aux[...]) * (IDX_DIM ** -0.5 * N_IDX_HEADS ** -0.5)
    gate = _dot(hb, w_gate[...])
    sg_o[...] = (1.0 / (1.0 + jnp.exp(-gate))).astype(BF16)


def _inproj(x2, pos, shift_scale, g_pre, w_in, bsz, seq, tm):
    n, d = x2.shape
    w_sb = N_HEADS_SB * HEAD_DIM
    w_sa = N_HEADS_SA * HEAD_DIM
    w_kv = N_KV_SA * HEAD_DIM
    w_ixq = N_IDX_HEADS * IDX_DIM
    cuts = np.cumsum([w_sb, w_sb, w_sb, w_sa, w_kv, w_kv, w_ixq, IDX_DIM, N_IDX_HEADS, d, d])
    c = [0] + [int(v) for v in cuts]
    wb = w_in.astype(BF16)
    seg = lambda i: wb[:, c[i]:c[i + 1]]
    wq_sb, wk_sb, wv_sb, wq_sa, wk_sa, wv_sa, wq_ix, wk_ix, w_ixw = [seg(i) for i in range(9)]
    w_gate = wb[:, c[9]:c[11]]
    dup = lambda w: jnp.concatenate(
        [w[:, j * HEAD_DIM:(j + 1) * HEAD_DIM] for j in range(w.shape[1] // HEAD_DIM) for _ in (0, 1)], axis=1)
    wkt_sb = wk_sb.T
    wkt_sa = dup(wk_sa).T
    wv_sa2 = dup(wv_sa)
    wkt_ix = dup(wk_ix).T
    w_aux = jnp.concatenate([w_ixw, jnp.zeros((d, LANES - N_IDX_HEADS), BF16)], axis=1)

    inv_freq = jnp.power(ROPE_THETA, -2.0 * jnp.arange(ROT_HALF, dtype=F32) / ROT_DIM)
    inv_lane = jnp.tile(inv_freq, LANES // ROT_HALF).reshape(1, LANES)
    inv_col = jnp.broadcast_to(inv_freq[:, None], (ROT_HALF, tm))

    nt = seq // tm
    row = lambda w: pl.BlockSpec((tm, w), lambda i: (i, 0))
    colT = lambda r: pl.BlockSpec((None, r, tm), lambda i: (i // nt, 0, i % nt))
    const = lambda a: pl.BlockSpec(a.shape, lambda i: (0,) * a.ndim)
    weights = [wq_sb, wkt_sb, wv_sb, wq_sa, wkt_sa, wv_sa2, wq_ix, wkt_ix, w_aux, w_gate]
    out_shape = [
        jax.ShapeDtypeStruct((n, w_sb), BF16),
        jax.ShapeDtypeStruct((bsz, w_sb, seq), BF16),
        jax.ShapeDtypeStruct((n, w_sb), BF16),
        jax.ShapeDtypeStruct((n, w_sa), BF16),
        jax.ShapeDtypeStruct((bsz, 2 * w_kv, seq), BF16),
        jax.ShapeDtypeStruct((n, 2 * w_kv), BF16),
        jax.ShapeDtypeStruct((n, w_ixq), BF16),
        jax.ShapeDtypeStruct((bsz, 2 * IDX_DIM, seq), BF16),
        jax.ShapeDtypeStruct((n, LANES), F32),
        jax.ShapeDtypeStruct((n, 2 * d), BF16),
    ]
    out_specs = [row(w_sb), colT(w_sb), row(w_sb), row(w_sa), colT(2 * w_kv), row(2 * w_kv),
                 row(w_ixq), colT(2 * IDX_DIM), row(LANES), row(2 * d)]
    return pl.pallas_call(
        _inproj_kernel,
        grid=(n // tm,),
        in_specs=[
            row(d),
            pl.BlockSpec((tm, 1), lambda i: (i, 0)),
            pl.BlockSpec((None, 1, tm), lambda i: (i // nt, 0, i % nt)),
            pl.BlockSpec((None, 1, d), lambda i: (i // nt, 0, 0)),
            pl.BlockSpec((None, 1, d), lambda i: (i // nt, 0, 1)),
            const(g_pre), const(inv_lane), const(inv_col),
        ] + [const(w) for w in weights],
        out_specs=out_specs,
        out_shape=out_shape,
        compiler_params=_cparams(("parallel",)),
        name="inproj",
    )(x2, pos.reshape(n, 1), pos.reshape(bsz, 1, seq), shift_scale, shift_scale, g_pre,
      inv_lane, inv_col, *weights)


def _softplus(z):
    return jnp.maximum(z, 0.0) + jnp.log(1.0 + jnp.exp(-jnp.abs(z)))


def _sb_kernel(q_ref, kt_ref, v_ref, o_ref, *, tq):
    qi = pl.program_id(2)
    q2 = q_ref[...]
    lane_half = lax.broadcasted_iota(jnp.int32, (1, LANES), 1) // HEAD_DIM
    r_i = lax.broadcasted_iota(jnp.int32, (tq, tq), 0)
    c_i = lax.broadcasted_iota(jnp.int32, (tq, tq), 1)
    strict = c_i < r_i
    ck = min(tq, 2 * LANES)
    later = jnp.where(lax.broadcasted_iota(jnp.int32, (ck, ck), 0) > lax.broadcasted_iota(jnp.int32, (ck, ck), 1),
                      1.0, 0.0).astype(BF16)
    qms = [jnp.where(lane_half == hh, q2, jnp.zeros_like(q2)) for hh in (0, 1)]

    def block(kb, state, diag):
        k0 = pl.multiple_of(kb * tq, tq)
        kt = kt_ref[:, pl.ds(k0, tq)]
        v = v_ref[pl.ds(k0, tq), :]
        new = []
        for hh in (0, 1):
            carry, acc = state[2 * hh], state[2 * hh + 1]
            z = _dot(qms[hh], kt)
            sp = _softplus(z)
            lf = -sp
            if diag:
                lf = jnp.where(strict, lf, 0.0)
            chunks = [None] * (tq // ck)
            for c in reversed(range(tq // ck)):
                lf_c = lf[:, c * ck:(c + 1) * ck]
                lf_hi, lf_lo = _split(lf_c)
                chunks[c] = _dot(lf_hi, later) + _dot(lf_lo, later) + carry
                carry = carry + jnp.sum(lf_c, axis=1, keepdims=True)
            between = jnp.concatenate(chunks, axis=1) if len(chunks) > 1 else chunks[0]
            a = jnp.exp((z - sp) + between)
            if diag:
                a = jnp.where(strict, a, 0.0)
            new += [carry, acc + _dot(a.astype(BF16), v)]
        return tuple(new)

    zero = (jnp.zeros((tq, 1), F32), jnp.zeros((tq, LANES), F32))
    state = block(qi, zero + zero, True)
    state = lax.fori_loop(0, qi, lambda it, st: block(qi - 1 - it, st, False), state)
    o_ref[...] = jnp.where(lane_half == 0, state[1], state[3]).astype(o_ref.dtype)


def _sb_attn(q, kt, v, bsz, seq, tq):
    n, w = q.shape
    npair = w // LANES
    nq = seq // tq
    return pl.pallas_call(
        functools.partial(_sb_kernel, tq=tq),
        grid=(bsz, npair, nq),
        in_specs=[
            pl.BlockSpec((tq, LANES), lambda b, p, i: (b * nq + i, p)),
            pl.BlockSpec((None, LANES, seq), lambda b, p, i: (b, p, 0)),
            pl.BlockSpec((seq, LANES), lambda b, p, i: (b, p)),
        ],
        out_specs=pl.BlockSpec((tq, LANES), lambda b, p, i: (b * nq + i, p)),
        out_shape=jax.ShapeDtypeStruct((n, w), BF16),
        compiler_params=_cparams(("parallel", "parallel", "arbitrary")),
        name="sb_attn",
    )(q, kt, v)


def _sort_key(s):
    bits = lax.bitcast_convert_type(s, jnp.int32)
    bits = jnp.where(bits == jnp.int32(-2 ** 31), 0, bits)
    return bits ^ ((bits >> 31) & jnp.int32(0x7FFFFFFF))


def _dsa_kernel(qix_ref, ktix_ref, wix_ref, q_ref, kt_ref, v_ref, o_ref,
                qm_scr, key_scr, bias_scr, qs_scr, m_scr, acc_scr, tau_scr, ngt_scr, *, tq, nq, k_sel):
    qi = pl.program_id(1)
    nkb = qi + 1
    lane_half = lax.broadcasted_iota(jnp.int32, (1, LANES), 1) // HEAD_DIM
    r_i = lax.broadcasted_iota(jnp.int32, (tq, tq), 0)
    c_i = lax.broadcasted_iota(jnp.int32, (tq, tq), 1)
    upto = jnp.where(r_i <= c_i, 1.0, 0.0).astype(BF16)

    for h in range(N_IDX_HEADS):
        qp = qix_ref[:, (h // 2) * LANES:(h // 2 + 1) * LANES]
        qm_scr[h] = jnp.where(lane_half == h % 2, qp, jnp.zeros_like(qp))

    def score_body(kb, _):
        k0 = pl.multiple_of(kb * tq, tq)
        kk = ktix_ref[:, pl.ds(k0, tq)]
        sc = jnp.zeros((tq, tq), F32)
        for h in range(N_IDX_HEADS):
            rel = jnp.maximum(_dot(qm_scr[h], kk), 0.0)
            sc = sc + rel * wix_ref[:, h:h + 1]
        causal = (c_i + k0) <= (r_i + qi * tq)
        sc = jnp.where(causal, sc, -jnp.inf)
        key_scr[:, pl.ds(k0, tq)] = _sort_key(sc)
        return 0

    lax.fori_loop(0, nkb, score_body, 0)

    n_grp = 4
    rg = tq // n_grp

    def search(nblk):
        def count_ge(g, cand):
            cand_b = jnp.broadcast_to(cand, (rg, LANES))
            part = jnp.zeros((rg, LANES), jnp.int32)
            for j in range(nblk * tq // LANES):
                blk = key_scr[g * rg:(g + 1) * rg, j * LANES:(j + 1) * LANES]
                part = part + jnp.where(blk >= cand_b, 1, 0)
            return jnp.sum(part, axis=1, keepdims=True)

        def bit_body(it, curs):
            bit = jnp.int32(1) << (31 - it)
            return tuple(jnp.where(count_ge(g, curs[g] + bit) >= k_sel, curs[g] + bit, curs[g])
                         for g in range(n_grp))

        lowest = jnp.full((rg, 1), -2 ** 31, jnp.int32)
        curs = lax.fori_loop(0, 32, bit_body, (lowest,) * n_grp)
        for g in range(n_grp):
            tau_scr[g * rg:(g + 1) * rg, :] = curs[g]
            ngt_scr[g * rg:(g + 1) * rg, :] = count_ge(g, curs[g] + 1)

    for q in range(nq):
        pl.when(qi == q)(functools.partial(search, q + 1))

    tau = tau_scr[...]
    need = (k_sel - ngt_scr[...]).astype(F32)

    def bias_body(kb, seen):
        k0 = pl.multiple_of(kb * tq, tq)
        key = key_scr[:, pl.ds(k0, tq)]
        tie = key == tau
        rank = _dot(jnp.where(tie, 1.0, 0.0).astype(BF16), upto) + seen
        sel = (key > tau) | (tie & (rank <= need))
        causal = (c_i + k0) <= (r_i + qi * tq)
        bias_scr[:, pl.ds(k0, tq)] = jnp.where(sel & causal, 0.0, NEG_BIG)
        return rank[:, tq - 1:tq]

    lax.fori_loop(0, nkb, bias_body, jnp.zeros((tq, 1), F32))

    gsz = N_HEADS_SA // N_KV_SA
    ones = jnp.ones((tq, LANES), BF16)
    for g in range(N_KV_SA):
        for hh in range(gsz):
            h = g * gsz + hh
            qp = q_ref[:, (h // 2) * LANES:(h // 2 + 1) * LANES]
            qs_scr[hh * tq:(hh + 1) * tq, :] = jnp.where(lane_half == h % 2, qp, jnp.zeros_like(qp))
        m_scr[...] = jnp.full(m_scr.shape, -jnp.inf, F32)
        acc_scr[...] = jnp.zeros(acc_scr.shape, F32)

        def masked_logits(kb, g=g):
            k0 = pl.multiple_of(kb * tq, tq)
            kk = kt_ref[g * LANES:(g + 1) * LANES, pl.ds(k0, tq)]
            logits = _dot(qs_scr[...], kk)
            return (logits.reshape(gsz, tq, tq) + bias_scr[:, pl.ds(k0, tq)][None]).reshape(gsz * tq, tq)

        def max_body(kb, _):
            logits = masked_logits(kb)
            part = m_scr[...]
            for j in range(tq // LANES):
                part = jnp.maximum(part, logits[:, j * LANES:(j + 1) * LANES])
            m_scr[...] = part
            return 0

        lax.fori_loop(0, nkb, max_body, 0)
        m_scr[...] = jnp.broadcast_to(jnp.max(m_scr[...], axis=1, keepdims=True), m_scr.shape)

        def attn_body(kb, _, g=g):
            k0 = pl.multiple_of(kb * tq, tq)
            logits = masked_logits(kb)
            mrow = m_scr[...]
            p = jnp.concatenate([jnp.exp(logits[:, j * LANES:(j + 1) * LANES] - mrow)
                                 for j in range(tq // LANES)], axis=1).astype(BF16)
            vv = jnp.concatenate([v_ref[pl.ds(k0, tq), g * LANES:(g + 1) * LANES], ones], axis=1)
            acc_scr[...] += _dot(p, vv)
            return 0

        lax.fori_loop(0, nkb, attn_body, 0)
        for hp in range(gsz // 2):
            pair = []
            for hh in (2 * hp, 2 * hp + 1):
                a = acc_scr[hh * tq:(hh + 1) * tq, :]
                pair.append(a[:, :LANES] / a[:, LANES:LANES + 1])
            col = (g * gsz // 2 + hp) * LANES
            o_ref[:, col:col + LANES] = jnp.where(lane_half == 0, pair[0], pair[1]).astype(o_ref.dtype)


def _dsa_attn(q_ix, kt_ix, w_ix, q, kt, v, bsz, seq, tq):
    n, w = q.shape
    nq = seq // tq
    k_sel = min(TOPK_MAX, seq // 4)
    gsz = N_HEADS_SA // N_KV_SA
    rowb = lambda width: pl.BlockSpec((tq, width), lambda b, i: (b * nq + i, 0))
    return pl.pallas_call(
        functools.partial(_dsa_kernel, tq=tq, nq=nq, k_sel=k_sel),
        grid=(bsz, nq),
        in_specs=[
            rowb(q_ix.shape[1]),
            pl.BlockSpec((None, kt_ix.shape[1], seq), lambda b, i: (b, 0, 0)),
            rowb(LANES),
            rowb(w),
            pl.BlockSpec((None, kt.shape[1], seq), lambda b, i: (b, 0, 0)),
            pl.BlockSpec((seq, v.shape[1]), lambda b, i: (b, 0)),
        ],
        out_specs=rowb(w),
        out_shape=jax.ShapeDtypeStruct((n, w), BF16),
        scratch_shapes=[
            pltpu.VMEM((N_IDX_HEADS, tq, LANES), BF16),
            pltpu.VMEM((tq, seq), jnp.int32),
            pltpu.VMEM((tq, seq), F32),
            pltpu.VMEM((gsz * tq, LANES), BF16),
            pltpu.VMEM((gsz * tq, LANES), F32),
            pltpu.VMEM((gsz * tq, 2 * LANES), F32),
            pltpu.VMEM((tq, 1), jnp.int32),
            pltpu.VMEM((tq, 1), jnp.int32),
        ],
        compiler_params=_cparams(("parallel", "arbitrary")),
        name="dsa_attn",
    )(q_ix, kt_ix, w_ix, q, kt, v)


def _put_row(stack, r, row):
    idx = lax.broadcasted_iota(jnp.int32, stack.shape, 0)
    return jnp.where(idx == r, row, stack)


def _top_desc(x, k):
    out = jnp.zeros((k, x.shape[1]), F32)
    for r in range(k):
        mx = jnp.max(x, axis=0, keepdims=True)
        out = _put_row(out, r, mx)
        x = jnp.where(x == mx, -jnp.inf, x)
    return out


def _merge_kernel(osb_ref, osa_ref, sg_ref, x_ref, gate1_ref, shift2_ref, scale2_ref,
                  gpost_ref, gpre_ref, wup_sb, wup_sa, wout, wqt, keys_ref,
                  x1_o, h2t_o, s_o, st_o, v2_o):
    d = x_ref.shape[1]
    y_sb = _dot(osb_ref[...], wup_sb[...])
    y_sa = _dot(osa_ref[...], wup_sa[...])
    sg = sg_ref[...].astype(F32)
    merged = sg[:, :d] * y_sb + sg[:, d:] * y_sa
    y = _dot(merged.astype(BF16), wout[...])
    x1 = x_ref[...] + gate1_ref[...] * _rms(y, gpost_ref[...])
    x1_o[...] = x1
    h2 = _rms(x1, gpre_ref[...]) * (1.0 + scale2_ref[...]) + shift2_ref[...]
    h2t = h2.T.astype(BF16)
    h2t_o[...] = h2t
    qt = _dot(wqt[...], h2t)

    half = PEER_QDIM // 2
    stats = jnp.zeros(st_o.shape, F32)
    for h in range(PEER_HEADS):
        tops = []
        for p in (0, 1):
            r0 = (2 * h + p) * half
            q_hi, q_lo = _split(qt[r0:r0 + half])
            kf = keys_ref[2 * h + p]
            k_hi, k_lo = _split(kf)
            s = _dot(k_hi, q_hi) + (_dot(k_hi, q_lo) + _dot(k_lo, q_hi))
            s_o[2 * h + p] = s
            tops.append(_top_desc(s, PEER_TOPK))
        v2_o[h * PEER_TOPK:(h + 1) * PEER_TOPK, :] = tops[1]
        hk = PEER_TOPK // 2
        cand = jnp.concatenate(
            [tops[0][0:1] + tops[1]]
            + [tops[0][k1:k1 + 1] + tops[1][0:hk] for k1 in range(1, hk)]
            + [tops[0][hk:] + tops[1][0:1]], axis=0)
        best = _top_desc(cand, PEER_TOPK)
        mx = best[0:1]
        z = jnp.sum(jnp.exp(best - mx), axis=0, keepdims=True)
        stats = _put_row(stats, h, best[PEER_TOPK - 1:PEER_TOPK])
        stats = _put_row(stats, PEER_HEADS + h, mx)
        stats = _put_row(stats, 2 * PEER_HEADS + h, 1.0 / z)
    st_o[...] = stats


def _merge(o_sb, o_sa, sg, x2, ada3, g_post, g_pre, w_up_sb, w_up_sa, w_out, w_query, sub_keys,
           seq, tm):
    n, d = x2.shape
    nt = seq // tm
    nk = 2 * PEER_HEADS
    wqt = w_query.T.astype(BF16)
    keys = sub_keys.reshape(nk, PEER_NKEYS, PEER_QDIM // 2)
    wup_sb, wup_sa, wout = w_up_sb.astype(BF16), w_up_sa.astype(BF16), w_out.astype(BF16)
    row = lambda w: pl.BlockSpec((tm, w), lambda i: (i, 0))
    const = lambda a: pl.BlockSpec(a.shape, lambda i: (0,) * a.ndim)
    adab = lambda j: pl.BlockSpec((None, 1, d), lambda i: (i // nt, 0, j))
    return pl.pallas_call(
        _merge_kernel,
        grid=(n // tm,),
        in_specs=[row(o_sb.shape[1]), row(o_sa.shape[1]), row(2 * d), row(d),
                  adab(2), adab(3), adab(4), const(g_post), const(g_pre),
                  const(wup_sb), const(wup_sa), const(wout), const(wqt), const(keys)],
        out_specs=[row(d),
                   pl.BlockSpec((d, tm), lambda i: (0, i)),
                   pl.BlockSpec((nk, PEER_NKEYS, tm), lambda i: (0, 0, i)),
                   pl.BlockSpec((4 * PEER_HEADS, tm), lambda i: (0, i)),
                   pl.BlockSpec((PEER_HEADS * PEER_TOPK, tm), lambda i: (0, i))],
        out_shape=[jax.ShapeDtypeStruct((n, d), F32),
                   jax.ShapeDtypeStruct((d, n), BF16),
                   jax.ShapeDtypeStruct((nk, PEER_NKEYS, n), F32),
                   jax.ShapeDtypeStruct((4 * PEER_HEADS, n), F32),
                   jax.ShapeDtypeStruct((PEER_HEADS * PEER_TOPK, n), F32)],
        compiler_params=_cparams(("parallel",)),
        name="merge",
    )(o_sb, o_sa, sg, x2, ada3, ada3, ada3, g_post, g_pre, wup_sb, wup_sa, wout, wqt, keys)


_ERFC_P = 0.3275911
_ERFC_C = (0.254829592, -0.284496736, 1.421413741, -1.453152027, 1.061405429)


def _gelu(a):
    u = jnp.abs(a)
    t = 1.0 / (1.0 + (_ERFC_P * 2.0 ** -0.5) * u)
    c = [0.5 * v for v in _ERFC_C]
    q = t * (c[0] + t * (c[1] + t * (c[2] + t * (c[3] + t * c[4])))) * jnp.exp(-0.5 * (a * a))
    return a * jnp.where(a >= 0.0, 1.0 - q, q)


BF16_ROWS = 16


def _pair_bits(x):
    u = lax.bitcast_convert_type(x.astype(BF16).astype(F32), jnp.uint32)
    return u | (u >> 16)


def _bcast_bf16(row_u32):
    tm = row_u32.shape[1]
    return pltpu.bitcast(jnp.broadcast_to(row_u32, (BF16_ROWS // 2, tm)), BF16)


def _peer_kernel(h2t_ref, win_ref, woutt_ref, s_ref, st_ref, v2_ref, x1_ref, gate2_ref, gpost_ref, o_ref,
                 acc_scr, act_scr, cnt_scr, e1_scr, rk_scr, e2_scr, *, te):
    k = pl.program_id(1)
    nk = PEER_NKEYS
    tm = h2t_ref.shape[1]

    @pl.when(k == 0)
    def _():
        acc_scr[...] = jnp.zeros(acc_scr.shape, F32)
        for h in range(PEER_HEADS):
            s1 = s_ref[2 * h]
            s2 = s_ref[2 * h + 1]
            tau = st_ref[h:h + 1, :]
            inv_z = st_ref[2 * PEER_HEADS + h:2 * PEER_HEADS + h + 1, :]
            cnt = jnp.zeros(s1.shape, F32)
            rank = jnp.zeros(s2.shape, F32)
            for k2 in range(PEER_TOPK):
                v = v2_ref[h * PEER_TOPK + k2:h * PEER_TOPK + k2 + 1, :]
                cnt = jnp.where(s1 + v >= tau, k2 + 1.0, cnt)
                rank = jnp.where(v > s2, k2 + 1.0, rank)
            cnt_scr[h] = _pair_bits(cnt)
            e1_scr[h] = _pair_bits(jnp.exp(s1 - jnp.max(s1, axis=0, keepdims=True)))
            rk_scr[h] = rank.astype(BF16)
            e2_scr[h] = (jnp.exp(s2 - v2_ref[h * PEER_TOPK:h * PEER_TOPK + 1, :]) * inv_z).astype(BF16)

    a = _dot(win_ref[...], h2t_ref[...])
    gel = _gelu(a)
    grp = nk // BF16_ROWS
    for ii in range(te // nk):
        i = k * (te // nk) + ii
        gate = jnp.zeros((grp, BF16_ROWS, tm), BF16)
        for h in range(PEER_HEADS):
            cnt = _bcast_bf16(cnt_scr[h, pl.ds(i, 1), :])[None]
            e1 = _bcast_bf16(e1_scr[h, pl.ds(i, 1), :])[None]
            rk = rk_scr[h].reshape(grp, BF16_ROWS, tm)
            e2 = e2_scr[h].reshape(grp, BF16_ROWS, tm)
            gate = gate + jnp.where(rk < cnt, e2, jnp.zeros_like(e2)) * e1
        act_scr[ii * nk:(ii + 1) * nk, :] = gate.reshape(nk, tm) * gel[ii * nk:(ii + 1) * nk].astype(BF16)
    acc_scr[...] += _dot(woutt_ref[...], act_scr[...])

    @pl.when(k == pl.num_programs(1) - 1)
    def _():
        o_ref[...] = x1_ref[...] + gate2_ref[...] * _rms(acc_scr[...].T, gpost_ref[...])


def _peer(h2t, w_in_e, w_out_e, s_all, stats, v2top, x1, ada3, g_post, seq, tm, te):
    d, n = h2t.shape
    nt = seq // tm
    n_exp = w_in_e.shape[0]
    win = w_in_e.astype(BF16)
    woutt = w_out_e.astype(BF16).reshape(n_exp // te, te, d).transpose(0, 2, 1)
    once = dict(pipeline_mode=pl.Buffered(1))
    tab_u = pltpu.VMEM((PEER_HEADS, PEER_NKEYS, tm), jnp.uint32)
    tab_b = pltpu.VMEM((PEER_HEADS, PEER_NKEYS, tm), BF16)
    return pl.pallas_call(
        functools.partial(_peer_kernel, te=te),
        grid=(n // tm, n_exp // te),
        in_specs=[
            pl.BlockSpec((d, tm), lambda i, k: (0, i), **once),
            pl.BlockSpec((te, d), lambda i, k: (k, 0)),
            pl.BlockSpec((None, d, te), lambda i, k: (k, 0, 0)),
            pl.BlockSpec((s_all.shape[0], PEER_NKEYS, tm), lambda i, k: (0, 0, i), **once),
            pl.BlockSpec((stats.shape[0], tm), lambda i, k: (0, i), **once),
            pl.BlockSpec((v2top.shape[0], tm), lambda i, k: (0, i), **once),
            pl.BlockSpec((tm, d), lambda i, k: (i, 0), **once),
            pl.BlockSpec((None, 1, d), lambda i, k: (i // nt, 0, 5)),
            pl.BlockSpec((1, d), lambda i, k: (0, 0)),
        ],
        out_specs=pl.BlockSpec((tm, d), lambda i, k: (i, 0)),
        out_shape=jax.ShapeDtypeStruct((n, d), F32),
        scratch_shapes=[pltpu.VMEM((d, tm), F32), pltpu.VMEM((te, tm), BF16), tab_u, tab_u, tab_b, tab_b],
        compiler_params=_cparams(("parallel", "arbitrary")),
        name="peer",
    )(h2t, win, woutt, s_all, stats, v2top, x1, ada3, g_post)


def _layer(x2, c, pos, bsz, seq, w_ada, b_ada, g_pre_mix, g_post_mix, w_in, w_up_sb, w_up_sa, w_out,
           g_pre_ffn, g_post_ffn, w_peer_query, peer_sub_keys, peer_expert_in, peer_expert_out):
    n, d = x2.shape
    tq = 256 if seq % 256 == 0 else 128
    tm = 256
    row1 = lambda g: g.reshape(1, d)
    ada3 = _ada(c, w_ada, b_ada).reshape(bsz, 1, 6 * d)
    (q_sb, kt_sb, v_sb, q_sa, kt_sa, v_sa, q_ix, kt_ix, w_ix, sg) = _inproj(
        x2, pos, ada3, row1(g_pre_mix), w_in, bsz, seq, tm)
    o_sb = _sb_attn(q_sb, kt_sb, v_sb, bsz, seq, 512 if seq % 512 == 0 else tq)
    o_sa = _dsa_attn(q_ix, kt_ix, w_ix, q_sa, kt_sa, v_sa, bsz, seq, tq)
    x1, h2t, s_all, stats, v2top = _merge(o_sb, o_sa, sg, x2, ada3, row1(g_post_mix), row1(g_pre_ffn),
                                          w_up_sb, w_up_sa, w_out, w_peer_query, peer_sub_keys, seq, tm)
    return _peer(h2t, peer_expert_in, peer_expert_out, s_all, stats, v2top, x1, ada3, row1(g_post_ffn),
                 seq, tm=min(1024, seq), te=512)


def kernel(x, c, positions, w_ada, b_ada, g_pre_mix, g_post_mix, w_in, w_up_sb, w_up_sa, w_out,
           g_pre_ffn, g_post_ffn, w_peer_query, peer_sub_keys, peer_expert_in, peer_expert_out):
    bsz, seq, d = x.shape
    x2 = x.reshape(bsz * seq, d)
    for l in range(w_ada.shape[0]):
        x2 = _layer(x2, c, positions, bsz, seq, w_ada[l], b_ada[l], g_pre_mix[l], g_post_mix[l],
                    w_in[l], w_up_sb[l], w_up_sa[l], w_out[l], g_pre_ffn[l], g_post_ffn[l],
                    w_peer_query[l], peer_sub_keys[l], peer_expert_in[l], peer_expert_out[l])
    return x2.reshape(bsz, seq, d)
```

```python
import functools
import math

import numpy as np
import jax
import jax.numpy as jnp
from jax import lax
from jax.experimental import pallas as pl
from jax.experimental.pallas import tpu as pltpu

HEAD_DIM = 64
N_HEADS_SB = 8
N_HEADS_SA = 8
N_KV_SA = 2
N_IDX_HEADS = 8
IDX_DIM = 64
TOPK_MAX = 256
ROPE_THETA = 500000.0
ROT_DIM = HEAD_DIM // 4
ROT_HALF = ROT_DIM // 2
PEER_HEADS = 8
PEER_NKEYS = 128
PEER_QDIM = 256
PEER_TOPK = 16
NORM_EPS = 1e-6

LANES = 128
NEG_BIG = -1e30
VMEM_LIMIT = 56 * 1024 * 1024

F32 = jnp.float32
BF16 = jnp.bfloat16


def _cparams(sem):
    return pltpu.CompilerParams(dimension_semantics=sem, vmem_limit_bytes=VMEM_LIMIT)


def _dot(a, b):
    return jnp.dot(a, b, preferred_element_type=F32)


def _dot_nt(a, b):
    return lax.dot_general(a, b, (((1,), (1,)), ((), ())), preferred_element_type=F32)


def _split(a):
    hi = a.astype(BF16)
    lo = (a - hi.astype(F32)).astype(BF16)
    return hi, lo


def _dot3(a, b):
    ah, al = _split(a)
    bh, bl = _split(b)
    return _dot(ah, bh) + (_dot(ah, bl) + _dot(al, bh))


def _rms(x, g):
    return x * lax.rsqrt(jnp.mean(x * x, axis=-1, keepdims=True) + NORM_EPS) * g


def _ada_kernel(c_ref, w_ref, b_ref, o_ref):
    c = c_ref[...]
    s = c / (1.0 + jnp.exp(-c))
    o_ref[...] = _dot3(s, w_ref[...]) + b_ref[...]


def _ada(c, w, b):
    bsz, d = c.shape
    n_out = w.shape[1]
    return pl.pallas_call(
        _ada_kernel,
        grid=(n_out // d,),
        in_specs=[
            pl.BlockSpec((bsz, d), lambda j: (0, 0)),
            pl.BlockSpec((d, d), lambda j: (0, j)),
            pl.BlockSpec((1, d), lambda j: (0, j)),
        ],
        out_specs=pl.BlockSpec((bsz, d), lambda j: (0, j)),
        out_shape=jax.ShapeDtypeStruct((bsz, n_out), F32),
        compiler_params=_cparams(("parallel",)),
        name="ada",
    )(c, w, b.reshape(1, n_out))


def _rope_rows(x, cos_t, sin_t):
    lane = lax.broadcasted_iota(jnp.int32, (1, LANES), 1) % HEAD_DIM
    c = jnp.where(lane < ROT_DIM, cos_t, 1.0)
    s_lo = jnp.where(lane < ROT_HALF, -sin_t, 0.0)
    s_hi = jnp.where((lane >= ROT_HALF) & (lane < ROT_DIM), sin_t, 0.0)
    outs = []
    for j in range(x.shape[1] // LANES):
        xb = x[:, j * LANES:(j + 1) * LANES]
        up = pltpu.roll(xb, LANES - ROT_HALF, 1)
        dn = pltpu.roll(xb, ROT_HALF, 1)
        outs.append(xb * c + up * s_lo + dn * s_hi)
    return jnp.concatenate(outs, axis=1) if len(outs) > 1 else outs[0]


def _rope_cols(x, cos_t, sin_t):
    outs = []
    for j in range(x.shape[0] // HEAD_DIM):
        blk = x[j * HEAD_DIM:(j + 1) * HEAD_DIM]
        x1 = blk[0:ROT_HALF]
        x2 = blk[ROT_HALF:ROT_DIM]
        outs += [x1 * cos_t - x2 * sin_t, x2 * cos_t + x1 * sin_t, blk[ROT_DIM:]]
    return jnp.concatenate(outs, axis=0)


def _inproj_kernel(x_ref, posc_ref, posr_ref, shift_ref, scale_ref, g_ref, invl_ref, invc_ref,
                   wq_sb, wkt_sb, wv_sb, wq_sa, wkt_sa, wv_sa, wq_ix, wkt_ix, w_aux, w_gate,
                   q_sb_o, kt_sb_o, v_sb_o, q_sa_o, kt_sa_o, v_sa_o, q_ix_o, kt_ix_o, wix_o, sg_o):
    x = x_ref[...]
    h = _rms(x, g_ref[...]) * (1.0 + scale_ref[...]) + shift_ref[...]
    hb = h.astype(BF16)

    ang_r = posc_ref[...].astype(F32) * invl_ref[...]
    cos_r, sin_r = jnp.cos(ang_r), jnp.sin(ang_r)
    ang_c = posr_ref[...].astype(F32) * invc_ref[...]
    cos_c, sin_c = jnp.cos(ang_c), jnp.sin(ang_c)

    scale = HEAD_DIM ** -0.5
    q_sb_o[...] = (_dot(hb, wq_sb[...]) * scale).astype(BF16)
    kt_sb_o[...] = _dot_nt(wkt_sb[...], hb).astype(BF16)
    v_sb_o[...] = _dot(hb, wv_sb[...]).astype(BF16)
    q_sa_o[...] = (_rope_rows(_dot(hb, wq_sa[...]), cos_r, sin_r) * scale).astype(BF16)
    kt_sa_o[...] = _rope_cols(_dot_nt(wkt_sa[...], hb), cos_c, sin_c).astype(BF16)
    v_sa_o[...] = _dot(hb, wv_sa[...]).astype(BF16)
    q_ix_o[...] = _rope_rows(_dot(hb, wq_ix[...]), cos_r, sin_r).astype(BF16)
    kt_ix_o[...] = _rope_cols(_dot_nt(wkt_ix[...], hb), cos_c, sin_c).astype(BF16)
    wix_o[...] = _dot(hb, w_aux[...]) * (IDX_DIM ** -0.5 * N_IDX_HEADS ** -0.5)
    gate = _dot(hb, w_gate[...])
    sg_o[...] = (1.0 / (1.0 + jnp.exp(-gate))).astype(BF16)


def _inproj(x2, pos, shift_scale, g_pre, w_in, bsz, seq, tm):
    n, d = x2.shape
    w_sb = N_HEADS_SB * HEAD_DIM
    w_sa = N_HEADS_SA * HEAD_DIM
    w_kv = N_KV_SA * HEAD_DIM
    w_ixq = N_IDX_HEADS * IDX_DIM
    cuts = np.cumsum([w_sb, w_sb, w_sb, w_sa, w_kv, w_kv, w_ixq, IDX_DIM, N_IDX_HEADS, d, d])
    c = [0] + [int(v) for v in cuts]
    wb = w_in.astype(BF16)
    seg = lambda i: wb[:, c[i]:c[i + 1]]
    wq_sb, wk_sb, wv_sb, wq_sa, wk_sa, wv_sa, wq_ix, wk_ix, w_ixw = [seg(i) for i in range(9)]
    w_gate = wb[:, c[9]:c[11]]
    dup = lambda w: jnp.concatenate(
        [w[:, j * HEAD_DIM:(j + 1) * HEAD_DIM] for j in range(w.shape[1] // HEAD_DIM) for _ in (0, 1)], axis=1)
    wkt_sb = wk_sb.T
    wkt_sa = dup(wk_sa).T
    wv_sa2 = dup(wv_sa)
    wkt_ix = dup(wk_ix).T
    w_aux = jnp.concatenate([w_ixw, jnp.zeros((d, LANES - N_IDX_HEADS), BF16)], axis=1)

    inv_freq = jnp.power(ROPE_THETA, -2.0 * jnp.arange(ROT_HALF, dtype=F32) / ROT_DIM)
    inv_lane = jnp.tile(inv_freq, LANES // ROT_HALF).reshape(1, LANES)
    inv_col = jnp.broadcast_to(inv_freq[:, None], (ROT_HALF, tm))

    nt = seq // tm
    row = lambda w: pl.BlockSpec((tm, w), lambda i: (i, 0))
    colT = lambda r: pl.BlockSpec((None, r, tm), lambda i: (i // nt, 0, i % nt))
    const = lambda a: pl.BlockSpec(a.shape, lambda i: (0,) * a.ndim)
    weights = [wq_sb, wkt_sb, wv_sb, wq_sa, wkt_sa, wv_sa2, wq_ix, wkt_ix, w_aux, w_gate]
    out_shape = [
        jax.ShapeDtypeStruct((n, w_sb), BF16),
        jax.ShapeDtypeStruct((bsz, w_sb, seq), BF16),
        jax.ShapeDtypeStruct((n, w_sb), BF16),
        jax.ShapeDtypeStruct((n, w_sa), BF16),
        jax.ShapeDtypeStruct((bsz, 2 * w_kv, seq), BF16),
        jax.ShapeDtypeStruct((n, 2 * w_kv), BF16),
        jax.ShapeDtypeStruct((n, w_ixq), BF16),
        jax.ShapeDtypeStruct((bsz, 2 * IDX_DIM, seq), BF16),
        jax.ShapeDtypeStruct((n, LANES), F32),
        jax.ShapeDtypeStruct((n, 2 * d), BF16),
    ]
    out_specs = [row(w_sb), colT(w_sb), row(w_sb), row(w_sa), colT(2 * w_kv), row(2 * w_kv),
                 row(w_ixq), colT(2 * IDX_DIM), row(LANES), row(2 * d)]
    return pl.pallas_call(
        _inproj_kernel,
        grid=(n // tm,),
        in_specs=[
            row(d),
            pl.BlockSpec((tm, 1), lambda i: (i, 0)),
            pl.BlockSpec((None, 1, tm), lambda i: (i // nt, 0, i % nt)),
            pl.BlockSpec((None, 1, d), lambda i: (i // nt, 0, 0)),
            pl.BlockSpec((None, 1, d), lambda i: (i // nt, 0, 1)),
            const(g_pre), const(inv_lane), const(inv_col),
        ] + [const(w) for w in weights],
        out_specs=out_specs,
        out_shape=out_shape,
        compiler_params=_cparams(("parallel",)),
        name="inproj",
    )(x2, pos.reshape(n, 1), pos.reshape(bsz, 1, seq), shift_scale, shift_scale, g_pre,
      inv_lane, inv_col, *weights)


def _softplus(z):
    return jnp.maximum(z, 0.0) + jnp.log(1.0 + jnp.exp(-jnp.abs(z)))


def _sb_kernel(q_ref, kt_ref, v_ref, o_ref, *, tq):
    qi = pl.program_id(2)
    q2 = q_ref[...]
    lane_half = lax.broadcasted_iota(jnp.int32, (1, LANES), 1) // HEAD_DIM
    r_i = lax.broadcasted_iota(jnp.int32, (tq, tq), 0)
    c_i = lax.broadcasted_iota(jnp.int32, (tq, tq), 1)
    strict = c_i < r_i
    ck = min(tq, 2 * LANES)
    later = jnp.where(lax.broadcasted_iota(jnp.int32, (ck, ck), 0) > lax.broadcasted_iota(jnp.int32, (ck, ck), 1),
                      1.0, 0.0).astype(BF16)
    qms = [jnp.where(lane_half == hh, q2, jnp.zeros_like(q2)) for hh in (0, 1)]

    def block(kb, state, diag):
        k0 = pl.multiple_of(kb * tq, tq)
        kt = kt_ref[:, pl.ds(k0, tq)]
        v = v_ref[pl.ds(k0, tq), :]
        new = []
        for hh in (0, 1):
            carry, acc = state[2 * hh], state[2 * hh + 1]
            z = _dot(qms[hh], kt)
            sp = _softplus(z)
            lf = -sp
            if diag:
                lf = jnp.where(strict, lf, 0.0)
            chunks = [None] * (tq // ck)
            for c in reversed(range(tq // ck)):
                lf_c = lf[:, c * ck:(c + 1) * ck]
                lf_hi, lf_lo = _split(lf_c)
                chunks[c] = _dot(lf_hi, later) + _dot(lf_lo, later) + carry
                carry = carry + jnp.sum(lf_c, axis=1, keepdims=True)
            between = jnp.concatenate(chunks, axis=1) if len(chunks) > 1 else chunks[0]
            a = jnp.exp((z - sp) + between)
            if diag:
                a = jnp.where(strict, a, 0.0)
            new += [carry, acc + _dot(a.astype(BF16), v)]
        return tuple(new)

    zero = (jnp.zeros((tq, 1), F32), jnp.zeros((tq, LANES), F32))
    state = block(qi, zero + zero, True)
    state = lax.fori_loop(0, qi, lambda it, st: block(qi - 1 - it, st, False), state)
    o_ref[...] = jnp.where(lane_half == 0, state[1], state[3]).astype(o_ref.dtype)


def _sb_attn(q, kt, v, bsz, seq, tq):
    n, w = q.shape
    npair = w // LANES
    nq = seq // tq
    return pl.pallas_call(
        functools.partial(_sb_kernel, tq=tq),
        grid=(bsz, npair, nq),
        in_specs=[
            pl.BlockSpec((tq, LANES), lambda b, p, i: (b * nq + i, p)),
            pl.BlockSpec((None, LANES, seq), lambda b, p, i: (b, p, 0)),
            pl.BlockSpec((seq, LANES), lambda b, p, i: (b, p)),
        ],
        out_specs=pl.BlockSpec((tq, LANES), lambda b, p, i: (b * nq + i, p)),
        out_shape=jax.ShapeDtypeStruct((n, w), BF16),
        compiler_params=_cparams(("parallel", "parallel", "arbitrary")),
        name="sb_attn",
    )(q, kt, v)


def _sort_key(s):
    bits = lax.bitcast_convert_type(s, jnp.int32)
    bits = jnp.where(bits == jnp.int32(-2 ** 31), 0, bits)
    return bits ^ ((bits >> 31) & jnp.int32(0x7FFFFFFF))


def _dsa_kernel(qix_ref, ktix_ref, wix_ref, q_ref, kt_ref, v_ref, o_ref,
                qm_scr, key_scr, bias_scr, qs_scr, m_scr, acc_scr, tau_scr, ngt_scr, *, tq, nq, k_sel):
    qi = pl.program_id(1)
    nkb = qi + 1
    lane_half = lax.broadcasted_iota(jnp.int32, (1, LANES), 1) // HEAD_DIM
    r_i = lax.broadcasted_iota(jnp.int32, (tq, tq), 0)
    c_i = lax.broadcasted_iota(jnp.int32, (tq, tq), 1)
    upto = jnp.where(r_i <= c_i, 1.0, 0.0).astype(BF16)

    for h in range(N_IDX_HEADS):
        qp = qix_ref[:, (h // 2) * LANES:(h // 2 + 1) * LANES]
        qm_scr[h] = jnp.where(lane_half == h % 2, qp, jnp.zeros_like(qp))

    def score_body(kb, _):
        k0 = pl.multiple_of(kb * tq, tq)
        kk = ktix_ref[:, pl.ds(k0, tq)]
        sc = jnp.zeros((tq, tq), F32)
        for h in range(N_IDX_HEADS):
            rel = jnp.maximum(_dot(qm_scr[h], kk), 0.0)
            sc = sc + rel * wix_ref[:, h:h + 1]
        causal = (c_i + k0) <= (r_i + qi * tq)
        sc = jnp.where(causal, sc, -jnp.inf)
        key_scr[:, pl.ds(k0, tq)] = _sort_key(sc)
        return 0

    lax.fori_loop(0, nkb, score_body, 0)

    n_grp = 4
    rg = tq // n_grp

    def search(nblk):
        def count_ge(g, cand):
            cand_b = jnp.broadcast_to(cand, (rg, LANES))
            part = jnp.zeros((rg, LANES), jnp.int32)
            for j in range(nblk * tq // LANES):
                blk = key_scr[g * rg:(g + 1) * rg, j * LANES:(j + 1) * LANES]
                part = part + jnp.where(blk >= cand_b, 1, 0)
            return jnp.sum(part, axis=1, keepdims=True)

        def bit_body(it, curs):
            bit = jnp.int32(1) << (31 - it)
            return tuple(jnp.where(count_ge(g, curs[g] + bit) >= k_sel, curs[g] + bit, curs[g])
                         for g in range(n_grp))

        lowest = jnp.full((rg, 1), -2 ** 31, jnp.int32)
        curs = lax.fori_loop(0, 32, bit_body, (lowest,) * n_grp)
        for g in range(n_grp):
            tau_scr[g * rg:(g + 1) * rg, :] = curs[g]
            ngt_scr[g * rg:(g + 1) * rg, :] = count_ge(g, curs[g] + 1)

    for q in range(nq):
        pl.when(qi == q)(functools.partial(search, q + 1))

    tau = tau_scr[...]
    need = (k_sel - ngt_scr[...]).astype(F32)

    def bias_body(kb, seen):
        k0 = pl.multiple_of(kb * tq, tq)
        key = key_scr[:, pl.ds(k0, tq)]
        tie = key == tau
        rank = _dot(jnp.where(tie, 1.0, 0.0).astype(BF16), upto) + seen
        sel = (key > tau) | (tie & (rank <= need))
        causal = (c_i + k0) <= (r_i + qi * tq)
        bias_scr[:, pl.ds(k0, tq)] = jnp.where(sel & causal, 0.0, NEG_BIG)
        return rank[:, tq - 1:tq]

    lax.fori_loop(0, nkb, bias_body, jnp.zeros((tq, 1), F32))

    gsz = N_HEADS_SA // N_KV_SA
    for g in range(N_KV_SA):
        for hh in range(gsz):
            h = g * gsz + hh
            qp = q_ref[:, (h // 2) * LANES:(h // 2 + 1) * LANES]
            qs_scr[hh * tq:(hh + 1) * tq, :] = jnp.where(lane_half == h % 2, qp, jnp.zeros_like(qp))
        m_scr[...] = jnp.full(m_scr.shape, -jnp.inf, F32)
        acc_scr[...] = jnp.zeros(acc_scr.shape, F32)

        def masked_logits(k0, w, g=g):
            kk = kt_ref[g * LANES:(g + 1) * LANES, pl.ds(k0, w)]
            logits = _dot(qs_scr[...], kk)
            return (logits.reshape(gsz, tq, w) + bias_scr[:, pl.ds(k0, w)][None]).reshape(gsz * tq, w)

        def max_step(k0, w):
            logits = masked_logits(k0, w)
            part = m_scr[...]
            for j in range(w // LANES):
                part = jnp.maximum(part, logits[:, j * LANES:(j + 1) * LANES])
            m_scr[...] = part

        def attn_step(k0, w, g=g):
            logits = masked_logits(k0, w)
            mrow = m_scr[...]
            p = jnp.concatenate([jnp.exp(logits[:, j * LANES:(j + 1) * LANES] - mrow)
                                 for j in range(w // LANES)], axis=1).astype(BF16)
            vv = jnp.concatenate([v_ref[pl.ds(k0, w), g * LANES:(g + 1) * LANES],
                                  jnp.ones((w, LANES), BF16)], axis=1)
            acc_scr[...] += _dot(p, vv)

        def sweep(step):
            def pair_body(it, _):
                step(pl.multiple_of(it * (2 * tq), 2 * tq), 2 * tq)
                return 0
            lax.fori_loop(0, nkb // 2, pair_body, 0)
            pl.when(nkb % 2 == 1)(lambda: step(pl.multiple_of((nkb - 1) * tq, tq), tq))

        sweep(max_step)
        m_scr[...] = jnp.broadcast_to(jnp.max(m_scr[...], axis=1, keepdims=True), m_scr.shape)
        sweep(attn_step)
        for hp in range(gsz // 2):
            pair = []
            for hh in (2 * hp, 2 * hp + 1):
                a = acc_scr[hh * tq:(hh + 1) * tq, :]
                pair.append(a[:, :LANES] / a[:, LANES:LANES + 1])
            col = (g * gsz // 2 + hp) * LANES
            o_ref[:, col:col + LANES] = jnp.where(lane_half == 0, pair[0], pair[1]).astype(o_ref.dtype)


def _dsa_attn(q_ix, kt_ix, w_ix, q, kt, v, bsz, seq, tq):
    n, w = q.shape
    nq = seq // tq
    k_sel = min(TOPK_MAX, seq // 4)
    gsz = N_HEADS_SA // N_KV_SA
    rowb = lambda width: pl.BlockSpec((tq, width), lambda b, i: (b * nq + i, 0))
    return pl.pallas_call(
        functools.partial(_dsa_kernel, tq=tq, nq=nq, k_sel=k_sel),
        grid=(bsz, nq),
        in_specs=[
            rowb(q_ix.shape[1]),
            pl.BlockSpec((None, kt_ix.shape[1], seq), lambda b, i: (b, 0, 0)),
            rowb(LANES),
            rowb(w),
            pl.BlockSpec((None, kt.shape[1], seq), lambda b, i: (b, 0, 0)),
            pl.BlockSpec((seq, v.shape[1]), lambda b, i: (b, 0)),
        ],
        out_specs=rowb(w),
        out_shape=jax.ShapeDtypeStruct((n, w), BF16),
        scratch_shapes=[
            pltpu.VMEM((N_IDX_HEADS, tq, LANES), BF16),
            pltpu.VMEM((tq, seq), jnp.int32),
            pltpu.VMEM((tq, seq), F32),
            pltpu.VMEM((gsz * tq, LANES), BF16),
            pltpu.VMEM((gsz * tq, LANES), F32),
            pltpu.VMEM((gsz * tq, 2 * LANES), F32),
            pltpu.VMEM((tq, 1), jnp.int32),
            pltpu.VMEM((tq, 1), jnp.int32),
        ],
        compiler_params=_cparams(("parallel", "arbitrary")),
        name="dsa_attn",
    )(q_ix, kt_ix, w_ix, q, kt, v)


def _put_row(stack, r, row):
    idx = lax.broadcasted_iota(jnp.int32, stack.shape, 0)
    return jnp.where(idx == r, row, stack)


def _top_desc(x, k):
    out = jnp.zeros((k, x.shape[1]), F32)
    for r in range(k):
        mx = jnp.max(x, axis=0, keepdims=True)
        out = _put_row(out, r, mx)
        x = jnp.where(x == mx, -jnp.inf, x)
    return out


def _merge_kernel(osb_ref, osa_ref, sg_ref, x_ref, gate1_ref, shift2_ref, scale2_ref,
                  gpost_ref, gpre_ref, wup_sb, wup_sa, wout, wqt, keys_ref,
                  x1_o, h2t_o, s_o, st_o, v2_o):
    d = x_ref.shape[1]
    y_sb = _dot(osb_ref[...], wup_sb[...])
    y_sa = _dot(osa_ref[...], wup_sa[...])
    sg = sg_ref[...].astype(F32)
    merged = sg[:, :d] * y_sb + sg[:, d:] * y_sa
    y = _dot(merged.astype(BF16), wout[...])
    x1 = x_ref[...] + gate1_ref[...] * _rms(y, gpost_ref[...])
    x1_o[...] = x1
    h2 = _rms(x1, gpre_ref[...]) * (1.0 + scale2_ref[...]) + shift2_ref[...]
    h2t = h2.T.astype(BF16)
    h2t_o[...] = h2t
    qt = _dot(wqt[...], h2t)

    half = PEER_QDIM // 2
    stats = jnp.zeros(st_o.shape, F32)
    for h in range(PEER_HEADS):
        tops = []
        for p in (0, 1):
            r0 = (2 * h + p) * half
            q_hi, q_lo = _split(qt[r0:r0 + half])
            kf = keys_ref[2 * h + p]
            k_hi, k_lo = _split(kf)
            s = _dot(k_hi, q_hi) + (_dot(k_hi, q_lo) + _dot(k_lo, q_hi))
            s_o[2 * h + p] = s
            tops.append(_top_desc(s, PEER_TOPK))
        v2_o[h * PEER_TOPK:(h + 1) * PEER_TOPK, :] = tops[1]
        hk = PEER_TOPK // 2
        cand = jnp.concatenate(
            [tops[0][0:1] + tops[1]]
            + [tops[0][k1:k1 + 1] + tops[1][0:hk] for k1 in range(1, hk)]
            + [tops[0][hk:] + tops[1][0:1]], axis=0)
        best = _top_desc(cand, PEER_TOPK)
        mx = best[0:1]
        z = jnp.sum(jnp.exp(best - mx), axis=0, keepdims=True)
        stats = _put_row(stats, h, best[PEER_TOPK - 1:PEER_TOPK])
        stats = _put_row(stats, PEER_HEADS + h, mx)
        stats = _put_row(stats, 2 * PEER_HEADS + h, 1.0 / z)
    st_o[...] = stats


def _merge(o_sb, o_sa, sg, x2, ada3, g_post, g_pre, w_up_sb, w_up_sa, w_out, w_query, sub_keys,
           seq, tm):
    n, d = x2.shape
    nt = seq // tm
    nk = 2 * PEER_HEADS
    wqt = w_query.T.astype(BF16)
    keys = sub_keys.reshape(nk, PEER_NKEYS, PEER_QDIM // 2)
    wup_sb, wup_sa, wout = w_up_sb.astype(BF16), w_up_sa.astype(BF16), w_out.astype(BF16)
    row = lambda w: pl.BlockSpec((tm, w), lambda i: (i, 0))
    const = lambda a: pl.BlockSpec(a.shape, lambda i: (0,) * a.ndim)
    adab = lambda j: pl.BlockSpec((None, 1, d), lambda i: (i // nt, 0, j))
    return pl.pallas_call(
        _merge_kernel,
        grid=(n // tm,),
        in_specs=[row(o_sb.shape[1]), row(o_sa.shape[1]), row(2 * d), row(d),
                  adab(2), adab(3), adab(4), const(g_post), const(g_pre),
                  const(wup_sb), const(wup_sa), const(wout), const(wqt), const(keys)],
        out_specs=[row(d),
                   pl.BlockSpec((d, tm), lambda i: (0, i)),
                   pl.BlockSpec((nk, PEER_NKEYS, tm), lambda i: (0, 0, i)),
                   pl.BlockSpec((4 * PEER_HEADS, tm), lambda i: (0, i)),
                   pl.BlockSpec((PEER_HEADS * PEER_TOPK, tm), lambda i: (0, i))],
        out_shape=[jax.ShapeDtypeStruct((n, d), F32),
                   jax.ShapeDtypeStruct((d, n), BF16),
                   jax.ShapeDtypeStruct((nk, PEER_NKEYS, n), F32),
                   jax.ShapeDtypeStruct((4 * PEER_HEADS, n), F32),
                   jax.ShapeDtypeStruct((PEER_HEADS * PEER_TOPK, n), F32)],
        compiler_params=_cparams(("parallel",)),
        name="merge",
    )(o_sb, o_sa, sg, x2, ada3, ada3, ada3, g_post, g_pre, wup_sb, wup_sa, wout, wqt, keys)


_ERFC_P = 0.3275911
_ERFC_C = (0.254829592, -0.284496736, 1.421413741, -1.453152027, 1.061405429)


def _gelu(a):
    u = jnp.abs(a)
    t = 1.0 / (1.0 + (_ERFC_P * 2.0 ** -0.5) * u)
    c = [0.5 * v for v in _ERFC_C]
    q = t * (c[0] + t * (c[1] + t * (c[2] + t * (c[3] + t * c[4])))) * jnp.exp2((-0.5 * math.log2(math.e)) * (a * a))
    return a * jnp.where(a >= 0.0, 1.0 - q, q)


BF16_ROWS = 16


def _pair_bits(x):
    u = lax.bitcast_convert_type(x.astype(BF16).astype(F32), jnp.uint32)
    return u | (u >> 16)


def _bcast_bf16(row_u32):
    tm = row_u32.shape[1]
    return pltpu.bitcast(jnp.broadcast_to(row_u32, (BF16_ROWS // 2, tm)), BF16)


def _peer_kernel(h2t_ref, win_ref, woutt_ref, s_ref, st_ref, v2_ref, x1_ref, gate2_ref, gpost_ref, o_ref,
                 acc_scr, act_scr, cnt_scr, e1_scr, rk_scr, e2_scr, *, te):
    k = pl.program_id(1)
    nk = PEER_NKEYS
    tm = h2t_ref.shape[1]

    @pl.when(k == 0)
    def _():
        acc_scr[...] = jnp.zeros(acc_scr.shape, F32)
        for h in range(PEER_HEADS):
            s1 = s_ref[2 * h]
            s2 = s_ref[2 * h + 1]
            tau = st_ref[h:h + 1, :]
            inv_z = st_ref[2 * PEER_HEADS + h:2 * PEER_HEADS + h + 1, :]
            cnt = jnp.zeros(s1.shape, F32)
            rank = jnp.zeros(s2.shape, F32)
            for k2 in range(PEER_TOPK):
                v = v2_ref[h * PEER_TOPK + k2:h * PEER_TOPK + k2 + 1, :]
                cnt = jnp.where(s1 + v >= tau, k2 + 1.0, cnt)
                rank = jnp.where(v > s2, k2 + 1.0, rank)
            cnt_scr[h] = _pair_bits(cnt)
            e1_scr[h] = _pair_bits(jnp.exp(s1 - jnp.max(s1, axis=0, keepdims=True)))
            rk_scr[h] = rank.astype(BF16)
            e2_scr[h] = (jnp.exp(s2 - v2_ref[h * PEER_TOPK:h * PEER_TOPK + 1, :]) * inv_z).astype(BF16)

    a = _dot(win_ref[...], h2t_ref[...])
    gel = _gelu(a)
    grp = nk // BF16_ROWS
    for ii in range(te // nk):
        i = k * (te // nk) + ii
        gate = jnp.zeros((grp, BF16_ROWS, tm), BF16)
        for h in range(PEER_HEADS):
            cnt = _bcast_bf16(cnt_scr[h, pl.ds(i, 1), :])[None]
            e1 = _bcast_bf16(e1_scr[h, pl.ds(i, 1), :])[None]
            rk = rk_scr[h].reshape(grp, BF16_ROWS, tm)
            e2 = e2_scr[h].reshape(grp, BF16_ROWS, tm)
            gate = gate + jnp.where(rk < cnt, e2, jnp.zeros_like(e2)) * e1
        act_scr[ii * nk:(ii + 1) * nk, :] = gate.reshape(nk, tm) * gel[ii * nk:(ii + 1) * nk].astype(BF16)
    acc_scr[...] += _dot(woutt_ref[...], act_scr[...])

    @pl.when(k == pl.num_programs(1) - 1)
    def _():
        o_ref[...] = x1_ref[...] + gate2_ref[...] * _rms(acc_scr[...].T, gpost_ref[...])


def _peer(h2t, w_in_e, w_out_e, s_all, stats, v2top, x1, ada3, g_post, seq, tm, te):
    d, n = h2t.shape
    nt = seq // tm
    n_exp = w_in_e.shape[0]
    win = w_in_e.astype(BF16)
    woutt = w_out_e.astype(BF16).reshape(n_exp // te, te, d).transpose(0, 2, 1)
    once = dict(pipeline_mode=pl.Buffered(1))
    tab_u = pltpu.VMEM((PEER_HEADS, PEER_NKEYS, tm), jnp.uint32)
    tab_b = pltpu.VMEM((PEER_HEADS, PEER_NKEYS, tm), BF16)
    return pl.pallas_call(
        functools.partial(_peer_kernel, te=te),
        grid=(n // tm, n_exp // te),
        in_specs=[
            pl.BlockSpec((d, tm), lambda i, k: (0, i), **once),
            pl.BlockSpec((te, d), lambda i, k: (k, 0)),
            pl.BlockSpec((None, d, te), lambda i, k: (k, 0, 0)),
            pl.BlockSpec((s_all.shape[0], PEER_NKEYS, tm), lambda i, k: (0, 0, i), **once),
            pl.BlockSpec((stats.shape[0], tm), lambda i, k: (0, i), **once),
            pl.BlockSpec((v2top.shape[0], tm), lambda i, k: (0, i), **once),
            pl.BlockSpec((tm, d), lambda i, k: (i, 0), **once),
            pl.BlockSpec((None, 1, d), lambda i, k: (i // nt, 0, 5)),
            pl.BlockSpec((1, d), lambda i, k: (0, 0)),
        ],
        out_specs=pl.BlockSpec((tm, d), lambda i, k: (i, 0)),
        out_shape=jax.ShapeDtypeStruct((n, d), F32),
        scratch_shapes=[pltpu.VMEM((d, tm), F32), pltpu.VMEM((te, tm), BF16), tab_u, tab_u, tab_b, tab_b],
        compiler_params=_cparams(("parallel", "arbitrary")),
        name="peer",
    )(h2t, win, woutt, s_all, stats, v2top, x1, ada3, g_post)


def _layer(x2, c, pos, bsz, seq, w_ada, b_ada, g_pre_mix, g_post_mix, w_in, w_up_sb, w_up_sa, w_out,
           g_pre_ffn, g_post_ffn, w_peer_query, peer_sub_keys, peer_expert_in, peer_expert_out):
    n, d = x2.shape
    tq = 256 if seq % 256 == 0 else 128
    tm = 256
    row1 = lambda g: g.reshape(1, d)
    ada3 = _ada(c, w_ada, b_ada).reshape(bsz, 1, 6 * d)
    (q_sb, kt_sb, v_sb, q_sa, kt_sa, v_sa, q_ix, kt_ix, w_ix, sg) = _inproj(
        x2, pos, ada3, row1(g_pre_mix), w_in, bsz, seq, tm)
    o_sb = _sb_attn(q_sb, kt_sb, v_sb, bsz, seq, 512 if seq % 512 == 0 else tq)
    o_sa = _dsa_attn(q_ix, kt_ix, w_ix, q_sa, kt_sa, v_sa, bsz, seq, tq)
    x1, h2t, s_all, stats, v2top = _merge(o_sb, o_sa, sg, x2, ada3, row1(g_post_mix), row1(g_pre_ffn),
                                          w_up_sb, w_up_sa, w_out, w_peer_query, peer_sub_keys, seq, tm)
    return _peer(h2t, peer_expert_in, peer_expert_out, s_all, stats, v2top, x1, ada3, row1(g_post_ffn),
                 seq, tm=min(1024, seq), te=512)


def kernel(x, c, positions, w_ada, b_ada, g_pre_mix, g_post_mix, w_in, w_up_sb, w_up_sa, w_out,
           g_pre_ffn, g_post_ffn, w_peer_query, peer_sub_keys, peer_expert_in, peer_expert_out):
    bsz, seq, d = x.shape
    x2 = x.reshape(bsz * seq, d)
    for l in range(w_ada.shape[0]):
        x2 = _layer(x2, c, positions, bsz, seq, w_ada[l], b_ada[l], g_pre_mix[l], g_post_mix[l],
                    w_in[l], w_up_sb[l], w_up_sa[l], w_out[l], g_pre_ffn[l], g_post_ffn[l],
                    w_peer_query[l], peer_sub_keys[l], peer_expert_in[l], peer_expert_out[l])
    return x2.reshape(bsz, seq, d)
```

```python
import functools
import math

import numpy as np
import jax
import jax.numpy as jnp
from jax import lax
from jax.experimental import pallas as pl
from jax.experimental.pallas import tpu as pltpu

HEAD_DIM = 64
N_HEADS_SB = 8
N_HEADS_SA = 8
N_KV_SA = 2
N_IDX_HEADS = 8
IDX_DIM = 64
TOPK_MAX = 256
ROPE_THETA = 500000.0
ROT_DIM = HEAD_DIM // 4
ROT_HALF = ROT_DIM // 2
PEER_HEADS = 8
PEER_NKEYS = 128
PEER_QDIM = 256
PEER_TOPK = 16
NORM_EPS = 1e-6

LANES = 128
NEG_BIG = -1e30
VMEM_LIMIT = 56 * 1024 * 1024

F32 = jnp.float32
BF16 = jnp.bfloat16


def _cparams(sem):
    return pltpu.CompilerParams(dimension_semantics=sem, vmem_limit_bytes=VMEM_LIMIT)


def _dot(a, b):
    return jnp.dot(a, b, preferred_element_type=F32)


def _dot_nt(a, b):
    return lax.dot_general(a, b, (((1,), (1,)), ((), ())), preferred_element_type=F32)


def _split(a):
    hi = a.astype(BF16)
    lo = (a - hi.astype(F32)).astype(BF16)
    return hi, lo


def _dot3(a, b):
    ah, al = _split(a)
    bh, bl = _split(b)
    return _dot(ah, bh) + (_dot(ah, bl) + _dot(al, bh))


def _rms(x, g):
    return x * lax.rsqrt(jnp.mean(x * x, axis=-1, keepdims=True) + NORM_EPS) * g


def _ada_kernel(c_ref, w_ref, b_ref, o_ref):
    c = c_ref[...]
    s = c / (1.0 + jnp.exp(-c))
    o_ref[...] = _dot3(s, w_ref[...]) + b_ref[...]


def _ada(c, w, b):
    bsz, d = c.shape
    n_out = w.shape[1]
    return pl.pallas_call(
        _ada_kernel,
        grid=(n_out // d,),
        in_specs=[
            pl.BlockSpec((bsz, d), lambda j: (0, 0)),
            pl.BlockSpec((d, d), lambda j: (0, j)),
            pl.BlockSpec((1, d), lambda j: (0, j)),
        ],
        out_specs=pl.BlockSpec((bsz, d), lambda j: (0, j)),
        out_shape=jax.ShapeDtypeStruct((bsz, n_out), F32),
        compiler_params=_cparams(("parallel",)),
        name="ada",
    )(c, w, b.reshape(1, n_out))


def _rope_rows(x, cos_t, sin_t):
    lane = lax.broadcasted_iota(jnp.int32, (1, LANES), 1) % HEAD_DIM
    c = jnp.where(lane < ROT_DIM, cos_t, 1.0)
    s_lo = jnp.where(lane < ROT_HALF, -sin_t, 0.0)
    s_hi = jnp.where((lane >= ROT_HALF) & (lane < ROT_DIM), sin_t, 0.0)
    outs = []
    for j in range(x.shape[1] // LANES):
        xb = x[:, j * LANES:(j + 1) * LANES]
        up = pltpu.roll(xb, LANES - ROT_HALF, 1)
        dn = pltpu.roll(xb, ROT_HALF, 1)
        outs.append(xb * c + up * s_lo + dn * s_hi)
    return jnp.concatenate(outs, axis=1) if len(outs) > 1 else outs[0]


def _rope_cols(x, cos_t, sin_t):
    outs = []
    for j in range(x.shape[0] // HEAD_DIM):
        blk = x[j * HEAD_DIM:(j + 1) * HEAD_DIM]
        x1 = blk[0:ROT_HALF]
        x2 = blk[ROT_HALF:ROT_DIM]
        outs += [x1 * cos_t - x2 * sin_t, x2 * cos_t + x1 * sin_t, blk[ROT_DIM:]]
    return jnp.concatenate(outs, axis=0)


def _inproj_kernel(x_ref, posc_ref, posr_ref, shift_ref, scale_ref, g_ref, invl_ref, invc_ref,
                   wq_sb, wkt_sb, wv_sb, wq_sa, wkt_sa, wv_sa, wq_ix, wkt_ix, w_aux, w_gate,
                   q_sb_o, kt_sb_o, v_sb_o, q_sa_o, kt_sa_o, v_sa_o, q_ix_o, kt_ix_o, wix_o, sg_o):
    x = x_ref[...]
    h = _rms(x, g_ref[...]) * (1.0 + scale_ref[...]) + shift_ref[...]
    hb = h.astype(BF16)

    ang_r = posc_ref[...].astype(F32) * invl_ref[...]
    cos_r, sin_r = jnp.cos(ang_r), jnp.sin(ang_r)
    ang_c = posr_ref[...].astype(F32) * invc_ref[...]
    cos_c, sin_c = jnp.cos(ang_c), jnp.sin(ang_c)

    scale = HEAD_DIM ** -0.5
    q_sb_o[...] = (_dot(hb, wq_sb[...]) * scale).astype(BF16)
    kt_sb_o[...] = _dot_nt(wkt_sb[...], hb).astype(BF16)
    v_sb_o[...] = _dot(hb, wv_sb[...]).astype(BF16)
    q_sa_o[...] = (_rope_rows(_dot(hb, wq_sa[...]), cos_r, sin_r) * scale).astype(BF16)
    kt_sa_o[...] = _rope_cols(_dot_nt(wkt_sa[...], hb), cos_c, sin_c).astype(BF16)
    v_sa_o[...] = _dot(hb, wv_sa[...]).astype(BF16)
    q_ix_o[...] = _rope_rows(_dot(hb, wq_ix[...]), cos_r, sin_r).astype(BF16)
    kt_ix_o[...] = _rope_cols(_dot_nt(wkt_ix[...], hb), cos_c, sin_c).astype(BF16)
    wix_o[...] = _dot(hb, w_aux[...]) * (IDX_DIM ** -0.5 * N_IDX_HEADS ** -0.5)
    gate = _dot(hb, w_gate[...])
    sg_o[...] = (1.0 / (1.0 + jnp.exp(-gate))).astype(BF16)


def _inproj(x2, pos, shift_scale, g_pre, w_in, bsz, seq, tm):
    n, d = x2.shape
    w_sb = N_HEADS_SB * HEAD_DIM
    w_sa = N_HEADS_SA * HEAD_DIM
    w_kv = N_KV_SA * HEAD_DIM
    w_ixq = N_IDX_HEADS * IDX_DIM
    cuts = np.cumsum([w_sb, w_sb, w_sb, w_sa, w_kv, w_kv, w_ixq, IDX_DIM, N_IDX_HEADS, d, d])
    c = [0] + [int(v) for v in cuts]
    wb = w_in.astype(BF16)
    seg = lambda i: wb[:, c[i]:c[i + 1]]
    wq_sb, wk_sb, wv_sb, wq_sa, wk_sa, wv_sa, wq_ix, wk_ix, w_ixw = [seg(i) for i in range(9)]
    w_gate = wb[:, c[9]:c[11]]
    dup = lambda w: jnp.concatenate(
        [w[:, j * HEAD_DIM:(j + 1) * HEAD_DIM] for j in range(w.shape[1] // HEAD_DIM) for _ in (0, 1)], axis=1)
    wkt_sb = wk_sb.T
    wkt_sa = dup(wk_sa).T
    wv_sa2 = dup(wv_sa)
    wkt_ix = dup(wk_ix).T
    w_aux = jnp.concatenate([w_ixw, jnp.zeros((d, LANES - N_IDX_HEADS), BF16)], axis=1)

    inv_freq = jnp.power(ROPE_THETA, -2.0 * jnp.arange(ROT_HALF, dtype=F32) / ROT_DIM)
    inv_lane = jnp.tile(inv_freq, LANES // ROT_HALF).reshape(1, LANES)
    inv_col = jnp.broadcast_to(inv_freq[:, None], (ROT_HALF, tm))

    nt = seq // tm
    row = lambda w: pl.BlockSpec((tm, w), lambda i: (i, 0))
    colT = lambda r: pl.BlockSpec((None, r, tm), lambda i: (i // nt, 0, i % nt))
    const = lambda a: pl.BlockSpec(a.shape, lambda i: (0,) * a.ndim)
    weights = [wq_sb, wkt_sb, wv_sb, wq_sa, wkt_sa, wv_sa2, wq_ix, wkt_ix, w_aux, w_gate]
    out_shape = [
        jax.ShapeDtypeStruct((n, w_sb), BF16),
        jax.ShapeDtypeStruct((bsz, w_sb, seq), BF16),
        jax.ShapeDtypeStruct((n, w_sb), BF16),
        jax.ShapeDtypeStruct((n, w_sa), BF16),
        jax.ShapeDtypeStruct((bsz, 2 * w_kv, seq), BF16),
        jax.ShapeDtypeStruct((n, 2 * w_kv), BF16),
        jax.ShapeDtypeStruct((n, w_ixq), BF16),
        jax.ShapeDtypeStruct((bsz, 2 * IDX_DIM, seq), BF16),
        jax.ShapeDtypeStruct((n, LANES), F32),
        jax.ShapeDtypeStruct((n, 2 * d), BF16),
    ]
    out_specs = [row(w_sb), colT(w_sb), row(w_sb), row(w_sa), colT(2 * w_kv), row(2 * w_kv),
                 row(w_ixq), colT(2 * IDX_DIM), row(LANES), row(2 * d)]
    return pl.pallas_call(
        _inproj_kernel,
        grid=(n // tm,),
        in_specs=[
            row(d),
            pl.BlockSpec((tm, 1), lambda i: (i, 0)),
            pl.BlockSpec((None, 1, tm), lambda i: (i // nt, 0, i % nt)),
            pl.BlockSpec((None, 1, d), lambda i: (i // nt, 0, 0)),
            pl.BlockSpec((None, 1, d), lambda i: (i // nt, 0, 1)),
            const(g_pre), const(inv_lane), const(inv_col),
        ] + [const(w) for w in weights],
        out_specs=out_specs,
        out_shape=out_shape,
        compiler_params=_cparams(("parallel",)),
        name="inproj",
    )(x2, pos.reshape(n, 1), pos.reshape(bsz, 1, seq), shift_scale, shift_scale, g_pre,
      inv_lane, inv_col, *weights)


def _softplus(z):
    return jnp.maximum(z, 0.0) + jnp.log(1.0 + jnp.exp(-jnp.abs(z)))


def _sb_kernel(q_ref, kt_ref, v_ref, o_ref, *, tq):
    qi = pl.program_id(2)
    q2 = q_ref[...]
    lane_half = lax.broadcasted_iota(jnp.int32, (1, LANES), 1) // HEAD_DIM
    r_i = lax.broadcasted_iota(jnp.int32, (tq, tq), 0)
    c_i = lax.broadcasted_iota(jnp.int32, (tq, tq), 1)
    strict = c_i < r_i
    ck = min(tq, 2 * LANES)
    later = jnp.where(lax.broadcasted_iota(jnp.int32, (ck, ck), 0) > lax.broadcasted_iota(jnp.int32, (ck, ck), 1),
                      1.0, 0.0).astype(BF16)
    qms = [jnp.where(lane_half == hh, q2, jnp.zeros_like(q2)) for hh in (0, 1)]

    def block(kb, state, diag):
        k0 = pl.multiple_of(kb * tq, tq)
        kt = kt_ref[:, pl.ds(k0, tq)]
        v = v_ref[pl.ds(k0, tq), :]
        new = []
        for hh in (0, 1):
            carry, acc = state[2 * hh], state[2 * hh + 1]
            z = _dot(qms[hh], kt)
            sp = _softplus(z)
            lf = -sp
            if diag:
                lf = jnp.where(strict, lf, 0.0)
            chunks = [None] * (tq // ck)
            for c in reversed(range(tq // ck)):
                lf_c = lf[:, c * ck:(c + 1) * ck]
                chunks[c] = _dot(lf_c.astype(BF16), later) + carry
                carry = carry + jnp.sum(lf_c, axis=1, keepdims=True)
            between = jnp.concatenate(chunks, axis=1) if len(chunks) > 1 else chunks[0]
            a = jnp.exp((z - sp) + between)
            if diag:
                a = jnp.where(strict, a, 0.0)
            new += [carry, acc + _dot(a.astype(BF16), v)]
        return tuple(new)

    zero = (jnp.zeros((tq, 1), F32), jnp.zeros((tq, LANES), F32))
    state = block(qi, zero + zero, True)
    state = lax.fori_loop(0, qi, lambda it, st: block(qi - 1 - it, st, False), state)
    o_ref[...] = jnp.where(lane_half == 0, state[1], state[3]).astype(o_ref.dtype)


def _sb_attn(q, kt, v, bsz, seq, tq):
    n, w = q.shape
    npair = w // LANES
    nq = seq // tq
    return pl.pallas_call(
        functools.partial(_sb_kernel, tq=tq),
        grid=(bsz, npair, nq),
        in_specs=[
            pl.BlockSpec((tq, LANES), lambda b, p, i: (b * nq + i, p)),
            pl.BlockSpec((None, LANES, seq), lambda b, p, i: (b, p, 0)),
            pl.BlockSpec((seq, LANES), lambda b, p, i: (b, p)),
        ],
        out_specs=pl.BlockSpec((tq, LANES), lambda b, p, i: (b * nq + i, p)),
        out_shape=jax.ShapeDtypeStruct((n, w), BF16),
        compiler_params=_cparams(("parallel", "parallel", "arbitrary")),
        name="sb_attn",
    )(q, kt, v)


def _sort_key(s):
    bits = lax.bitcast_convert_type(s, jnp.int32)
    bits = jnp.where(bits == jnp.int32(-2 ** 31), 0, bits)
    return bits ^ ((bits >> 31) & jnp.int32(0x7FFFFFFF))


def _dsa_kernel(qix_ref, ktix_ref, wix_ref, q_ref, kt_ref, v_ref, o_ref,
                qm_scr, key_scr, bias_scr, qs_scr, m_scr, acc_scr, tau_scr, ngt_scr, *, tq, nq, k_sel):
    qi = pl.program_id(1)
    nkb = qi + 1
    lane_half = lax.broadcasted_iota(jnp.int32, (1, LANES), 1) // HEAD_DIM
    r_i = lax.broadcasted_iota(jnp.int32, (tq, tq), 0)
    c_i = lax.broadcasted_iota(jnp.int32, (tq, tq), 1)
    upto = jnp.where(r_i <= c_i, 1.0, 0.0).astype(BF16)

    for h in range(N_IDX_HEADS):
        qp = qix_ref[:, (h // 2) * LANES:(h // 2 + 1) * LANES]
        qm_scr[h] = jnp.where(lane_half == h % 2, qp, jnp.zeros_like(qp))

    def score_body(kb, _):
        k0 = pl.multiple_of(kb * tq, tq)
        kk = ktix_ref[:, pl.ds(k0, tq)]
        sc = jnp.zeros((tq, tq), F32)
        for h in range(N_IDX_HEADS):
            rel = jnp.maximum(_dot(qm_scr[h], kk), 0.0)
            sc = sc + rel * wix_ref[:, h:h + 1]
        causal = (c_i + k0) <= (r_i + qi * tq)
        sc = jnp.where(causal, sc, -jnp.inf)
        key_scr[:, pl.ds(k0, tq)] = _sort_key(sc)
        return 0

    lax.fori_loop(0, nkb, score_body, 0)

    n_grp = 4
    rg = tq // n_grp

    def search(nblk):
        def count_ge(g, cand):
            cand_b = jnp.broadcast_to(cand, (rg, LANES))
            part = jnp.zeros((rg, LANES), jnp.int32)
            for j in range(nblk * tq // LANES):
                blk = key_scr[g * rg:(g + 1) * rg, j * LANES:(j + 1) * LANES]
                part = part + jnp.where(blk >= cand_b, 1, 0)
            return jnp.sum(part, axis=1, keepdims=True)

        def bit_body(it, curs):
            bit = jnp.int32(1) << (31 - it)
            return tuple(jnp.where(count_ge(g, curs[g] + bit) >= k_sel, curs[g] + bit, curs[g])
                         for g in range(n_grp))

        lowest = jnp.full((rg, 1), -2 ** 31, jnp.int32)
        curs = lax.fori_loop(0, 32, bit_body, (lowest,) * n_grp)
        for g in range(n_grp):
            tau_scr[g * rg:(g + 1) * rg, :] = curs[g]
            ngt_scr[g * rg:(g + 1) * rg, :] = count_ge(g, curs[g] + 1)

    for q in range(nq):
        pl.when(qi == q)(functools.partial(search, q + 1))

    tau = tau_scr[...]
    need = (k_sel - ngt_scr[...]).astype(F32)

    def bias_body(kb, seen):
        k0 = pl.multiple_of(kb * tq, tq)
        key = key_scr[:, pl.ds(k0, tq)]
        tie = key == tau
        rank = _dot(jnp.where(tie, 1.0, 0.0).astype(BF16), upto) + seen
        sel = (key > tau) | (tie & (rank <= need))
        causal = (c_i + k0) <= (r_i + qi * tq)
        bias_scr[:, pl.ds(k0, tq)] = jnp.where(sel & causal, 0.0, NEG_BIG)
        return rank[:, tq - 1:tq]

    lax.fori_loop(0, nkb, bias_body, jnp.zeros((tq, 1), F32))

    gsz = N_HEADS_SA // N_KV_SA
    for g in range(N_KV_SA):
        for hh in range(gsz):
            h = g * gsz + hh
            qp = q_ref[:, (h // 2) * LANES:(h // 2 + 1) * LANES]
            qs_scr[hh * tq:(hh + 1) * tq, :] = jnp.where(lane_half == h % 2, qp, jnp.zeros_like(qp))
        m_scr[...] = jnp.full(m_scr.shape, -jnp.inf, F32)
        acc_scr[...] = jnp.zeros(acc_scr.shape, F32)

        def masked_logits(k0, w, g=g):
            kk = kt_ref[g * LANES:(g + 1) * LANES, pl.ds(k0, w)]
            logits = _dot(qs_scr[...], kk)
            return (logits.reshape(gsz, tq, w) + bias_scr[:, pl.ds(k0, w)][None]).reshape(gsz * tq, w)

        def max_step(k0, w):
            logits = masked_logits(k0, w)
            part = m_scr[...]
            for j in range(w // LANES):
                part = jnp.maximum(part, logits[:, j * LANES:(j + 1) * LANES])
            m_scr[...] = part

        def attn_step(k0, w, g=g):
            logits = masked_logits(k0, w)
            mrow = m_scr[...]
            p = jnp.concatenate([jnp.exp(logits[:, j * LANES:(j + 1) * LANES] - mrow)
                                 for j in range(w // LANES)], axis=1).astype(BF16)
            vv = jnp.concatenate([v_ref[pl.ds(k0, w), g * LANES:(g + 1) * LANES],
                                  jnp.ones((w, LANES), BF16)], axis=1)
            acc_scr[...] += _dot(p, vv)

        def sweep(step):
            def pair_body(it, _):
                step(pl.multiple_of(it * (2 * tq), 2 * tq), 2 * tq)
                return 0
            lax.fori_loop(0, nkb // 2, pair_body, 0)
            pl.when(nkb % 2 == 1)(lambda: step(pl.multiple_of((nkb - 1) * tq, tq), tq))

        sweep(max_step)
        m_scr[...] = jnp.broadcast_to(jnp.max(m_scr[...], axis=1, keepdims=True), m_scr.shape)
        sweep(attn_step)
        for hp in range(gsz // 2):
            pair = []
            for hh in (2 * hp, 2 * hp + 1):
                a = acc_scr[hh * tq:(hh + 1) * tq, :]
                pair.append(a[:, :LANES] / a[:, LANES:LANES + 1])
            col = (g * gsz // 2 + hp) * LANES
            o_ref[:, col:col + LANES] = jnp.where(lane_half == 0, pair[0], pair[1]).astype(o_ref.dtype)


def _dsa_attn(q_ix, kt_ix, w_ix, q, kt, v, bsz, seq, tq):
    n, w = q.shape
    nq = seq // tq
    k_sel = min(TOPK_MAX, seq // 4)
    gsz = N_HEADS_SA // N_KV_SA
    rowb = lambda width: pl.BlockSpec((tq, width), lambda b, i: (b * nq + i, 0))
    return pl.pallas_call(
        functools.partial(_dsa_kernel, tq=tq, nq=nq, k_sel=k_sel),
        grid=(bsz, nq),
        in_specs=[
            rowb(q_ix.shape[1]),
            pl.BlockSpec((None, kt_ix.shape[1], seq), lambda b, i: (b, 0, 0)),
            rowb(LANES),
            rowb(w),
            pl.BlockSpec((None, kt.shape[1], seq), lambda b, i: (b, 0, 0)),
            pl.BlockSpec((seq, v.shape[1]), lambda b, i: (b, 0)),
        ],
        out_specs=rowb(w),
        out_shape=jax.ShapeDtypeStruct((n, w), BF16),
        scratch_shapes=[
            pltpu.VMEM((N_IDX_HEADS, tq, LANES), BF16),
            pltpu.VMEM((tq, seq), jnp.int32),
            pltpu.VMEM((tq, seq), F32),
            pltpu.VMEM((gsz * tq, LANES), BF16),
            pltpu.VMEM((gsz * tq, LANES), F32),
            pltpu.VMEM((gsz * tq, 2 * LANES), F32),
            pltpu.VMEM((tq, 1), jnp.int32),
            pltpu.VMEM((tq, 1), jnp.int32),
        ],
        compiler_params=_cparams(("parallel", "arbitrary")),
        name="dsa_attn",
    )(q_ix, kt_ix, w_ix, q, kt, v)


def _put_row(stack, r, row):
    idx = lax.broadcasted_iota(jnp.int32, stack.shape, 0)
    return jnp.where(idx == r, row, stack)


def _top_desc(x, k):
    out = jnp.zeros((k, x.shape[1]), F32)
    for r in range(k):
        mx = jnp.max(x, axis=0, keepdims=True)
        out = _put_row(out, r, mx)
        x = jnp.where(x == mx, -jnp.inf, x)
    return out


def _merge_kernel(osb_ref, osa_ref, sg_ref, x_ref, gate1_ref, shift2_ref, scale2_ref,
                  gpost_ref, gpre_ref, wup_sb, wup_sa, wout, wqt, keys_ref,
                  x1_o, h2t_o, s_o, st_o, v2_o):
    d = x_ref.shape[1]
    y_sb = _dot(osb_ref[...], wup_sb[...])
    y_sa = _dot(osa_ref[...], wup_sa[...])
    sg = sg_ref[...].astype(F32)
    merged = sg[:, :d] * y_sb + sg[:, d:] * y_sa
    y = _dot(merged.astype(BF16), wout[...])
    x1 = x_ref[...] + gate1_ref[...] * _rms(y, gpost_ref[...])
    x1_o[...] = x1
    h2 = _rms(x1, gpre_ref[...]) * (1.0 + scale2_ref[...]) + shift2_ref[...]
    h2t = h2.T.astype(BF16)
    h2t_o[...] = h2t
    qt = _dot(wqt[...], h2t)

    half = PEER_QDIM // 2
    stats = jnp.zeros(st_o.shape, F32)
    for h in range(PEER_HEADS):
        tops = []
        for p in (0, 1):
            r0 = (2 * h + p) * half
            q_hi, q_lo = _split(qt[r0:r0 + half])
            kf = keys_ref[2 * h + p]
            k_hi, k_lo = _split(kf)
            s = _dot(k_hi, q_hi) + (_dot(k_hi, q_lo) + _dot(k_lo, q_hi))
            s_o[2 * h + p] = s
            tops.append(_top_desc(s, PEER_TOPK))
        v2_o[h * PEER_TOPK:(h + 1) * PEER_TOPK, :] = tops[1]
        hk = PEER_TOPK // 2
        cand = jnp.concatenate(
            [tops[0][0:1] + tops[1]]
            + [tops[0][k1:k1 + 1] + tops[1][0:hk] for k1 in range(1, hk)]
            + [tops[0][hk:] + tops[1][0:1]], axis=0)
        best = _top_desc(cand, PEER_TOPK)
        mx = best[0:1]
        z = jnp.sum(jnp.exp(best - mx), axis=0, keepdims=True)
        stats = _put_row(stats, h, best[PEER_TOPK - 1:PEER_TOPK])
        stats = _put_row(stats, PEER_HEADS + h, mx)
        stats = _put_row(stats, 2 * PEER_HEADS + h, 1.0 / z)
    st_o[...] = stats


def _merge(o_sb, o_sa, sg, x2, ada3, g_post, g_pre, w_up_sb, w_up_sa, w_out, w_query, sub_keys,
           seq, tm):
    n, d = x2.shape
    nt = seq // tm
    nk = 2 * PEER_HEADS
    wqt = w_query.T.astype(BF16)
    keys = sub_keys.reshape(nk, PEER_NKEYS, PEER_QDIM // 2)
    wup_sb, wup_sa, wout = w_up_sb.astype(BF16), w_up_sa.astype(BF16), w_out.astype(BF16)
    row = lambda w: pl.BlockSpec((tm, w), lambda i: (i, 0))
    const = lambda a: pl.BlockSpec(a.shape, lambda i: (0,) * a.ndim)
    adab = lambda j: pl.BlockSpec((None, 1, d), lambda i: (i // nt, 0, j))
    return pl.pallas_call(
        _merge_kernel,
        grid=(n // tm,),
        in_specs=[row(o_sb.shape[1]), row(o_sa.shape[1]), row(2 * d), row(d),
                  adab(2), adab(3), adab(4), const(g_post), const(g_pre),
                  const(wup_sb), const(wup_sa), const(wout), const(wqt), const(keys)],
        out_specs=[row(d),
                   pl.BlockSpec((d, tm), lambda i: (0, i)),
                   pl.BlockSpec((nk, PEER_NKEYS, tm), lambda i: (0, 0, i)),
                   pl.BlockSpec((4 * PEER_HEADS, tm), lambda i: (0, i)),
                   pl.BlockSpec((PEER_HEADS * PEER_TOPK, tm), lambda i: (0, i))],
        out_shape=[jax.ShapeDtypeStruct((n, d), F32),
                   jax.ShapeDtypeStruct((d, n), BF16),
                   jax.ShapeDtypeStruct((nk, PEER_NKEYS, n), F32),
                   jax.ShapeDtypeStruct((4 * PEER_HEADS, n), F32),
                   jax.ShapeDtypeStruct((PEER_HEADS * PEER_TOPK, n), F32)],
        compiler_params=_cparams(("parallel",)),
        name="merge",
    )(o_sb, o_sa, sg, x2, ada3, ada3, ada3, g_post, g_pre, wup_sb, wup_sa, wout, wqt, keys)


_ERFC_P = 0.3275911
_ERFC_C = (0.254829592, -0.284496736, 1.421413741, -1.453152027, 1.061405429)


def _gelu(a):
    u = jnp.abs(a)
    t = 1.0 / (1.0 + (_ERFC_P * 2.0 ** -0.5) * u)
    c = [0.5 * v for v in _ERFC_C]
    q = t * (c[0] + t * (c[1] + t * (c[2] + t * (c[3] + t * c[4])))) * jnp.exp2((-0.5 * math.log2(math.e)) * (a * a))
    return a * jnp.where(a >= 0.0, 1.0 - q, q)


BF16_ROWS = 16


def _pair_bits(x):
    u = lax.bitcast_convert_type(x.astype(BF16).astype(F32), jnp.uint32)
    return u | (u >> 16)


def _bcast_bf16(row_u32):
    tm = row_u32.shape[1]
    return pltpu.bitcast(jnp.broadcast_to(row_u32, (BF16_ROWS // 2, tm)), BF16)


def _peer_kernel(h2t_ref, win_ref, woutt_ref, s_ref, st_ref, v2_ref, x1_ref, gate2_ref, gpost_ref, o_ref,
                 acc_scr, act_scr, cnt_scr, e1_scr, rk_scr, e2_scr, *, te):
    k = pl.program_id(1)
    nk = PEER_NKEYS
    tm = h2t_ref.shape[1]

    @pl.when(k == 0)
    def _():
        acc_scr[...] = jnp.zeros(acc_scr.shape, F32)
        for h in range(PEER_HEADS):
            s1 = s_ref[2 * h]
            s2 = s_ref[2 * h + 1]
            tau = st_ref[h:h + 1, :]
            inv_z = st_ref[2 * PEER_HEADS + h:2 * PEER_HEADS + h + 1, :]
            cnt = jnp.zeros(s1.shape, F32)
            rank = jnp.zeros(s2.shape, F32)
            for k2 in range(PEER_TOPK):
                v = v2_ref[h * PEER_TOPK + k2:h * PEER_TOPK + k2 + 1, :]
                cnt = jnp.where(s1 + v >= tau, k2 + 1.0, cnt)
                rank = jnp.where(v > s2, k2 + 1.0, rank)
            cnt_scr[h] = _pair_bits(cnt)
            e1_scr[h] = _pair_bits(jnp.exp(s1 - jnp.max(s1, axis=0, keepdims=True)))
            rk_scr[h] = rank.astype(BF16)
            e2_scr[h] = (jnp.exp(s2 - v2_ref[h * PEER_TOPK:h * PEER_TOPK + 1, :]) * inv_z).astype(BF16)

    a = _dot(win_ref[...], h2t_ref[...])
    gel = _gelu(a)
    grp = nk // BF16_ROWS
    for ii in range(te // nk):
        i = k * (te // nk) + ii
        gate = jnp.zeros((grp, BF16_ROWS, tm), BF16)
        for h in range(PEER_HEADS):
            cnt = _bcast_bf16(cnt_scr[h, pl.ds(i, 1), :])[None]
            e1 = _bcast_bf16(e1_scr[h, pl.ds(i, 1), :])[None]
            rk = rk_scr[h].reshape(grp, BF16_ROWS, tm)
            e2 = e2_scr[h].reshape(grp, BF16_ROWS, tm)
            gate = gate + jnp.where(rk < cnt, e2, jnp.zeros_like(e2)) * e1
        act_scr[ii * nk:(ii + 1) * nk, :] = gate.reshape(nk, tm) * gel[ii * nk:(ii + 1) * nk].astype(BF16)
    acc_scr[...] += _dot(woutt_ref[...], act_scr[...])

    @pl.when(k == pl.num_programs(1) - 1)
    def _():
        o_ref[...] = x1_ref[...] + gate2_ref[...] * _rms(acc_scr[...].T, gpost_ref[...])


def _peer(h2t, w_in_e, w_out_e, s_all, stats, v2top, x1, ada3, g_post, seq, tm, te):
    d, n = h2t.shape
    nt = seq // tm
    n_exp = w_in_e.shape[0]
    win = w_in_e.astype(BF16)
    woutt = w_out_e.astype(BF16).reshape(n_exp // te, te, d).transpose(0, 2, 1)
    once = dict(pipeline_mode=pl.Buffered(1))
    tab_u = pltpu.VMEM((PEER_HEADS, PEER_NKEYS, tm), jnp.uint32)
    tab_b = pltpu.VMEM((PEER_HEADS, PEER_NKEYS, tm), BF16)
    return pl.pallas_call(
        functools.partial(_peer_kernel, te=te),
        grid=(n // tm, n_exp // te),
        in_specs=[
            pl.BlockSpec((d, tm), lambda i, k: (0, i), **once),
            pl.BlockSpec((te, d), lambda i, k: (k, 0)),
            pl.BlockSpec((None, d, te), lambda i, k: (k, 0, 0)),
            pl.BlockSpec((s_all.shape[0], PEER_NKEYS, tm), lambda i, k: (0, 0, i), **once),
            pl.BlockSpec((stats.shape[0], tm), lambda i, k: (0, i), **once),
            pl.BlockSpec((v2top.shape[0], tm), lambda i, k: (0, i), **once),
            pl.BlockSpec((tm, d), lambda i, k: (i, 0), **once),
            pl.BlockSpec((None, 1, d), lambda i, k: (i // nt, 0, 5)),
            pl.BlockSpec((1, d), lambda i, k: (0, 0)),
        ],
        out_specs=pl.BlockSpec((tm, d), lambda i, k: (i, 0)),
        out_shape=jax.ShapeDtypeStruct((n, d), F32),
        scratch_shapes=[pltpu.VMEM((d, tm), F32), pltpu.VMEM((te, tm), BF16), tab_u, tab_u, tab_b, tab_b],
        compiler_params=_cparams(("parallel", "arbitrary")),
        name="peer",
    )(h2t, win, woutt, s_all, stats, v2top, x1, ada3, g_post)


def _layer(x2, c, pos, bsz, seq, w_ada, b_ada, g_pre_mix, g_post_mix, w_in, w_up_sb, w_up_sa, w_out,
           g_pre_ffn, g_post_ffn, w_peer_query, peer_sub_keys, peer_expert_in, peer_expert_out):
    n, d = x2.shape
    tq = 256 if seq % 256 == 0 else 128
    tm = 256
    row1 = lambda g: g.reshape(1, d)
    ada3 = _ada(c, w_ada, b_ada).reshape(bsz, 1, 6 * d)
    (q_sb, kt_sb, v_sb, q_sa, kt_sa, v_sa, q_ix, kt_ix, w_ix, sg) = _inproj(
        x2, pos, ada3, row1(g_pre_mix), w_in, bsz, seq, 512 if seq % 512 == 0 else tm)
    o_sb = _sb_attn(q_sb, kt_sb, v_sb, bsz, seq, 512 if seq % 512 == 0 else tq)
    o_sa = _dsa_attn(q_ix, kt_ix, w_ix, q_sa, kt_sa, v_sa, bsz, seq, tq)
    x1, h2t, s_all, stats, v2top = _merge(o_sb, o_sa, sg, x2, ada3, row1(g_post_mix), row1(g_pre_ffn),
                                          w_up_sb, w_up_sa, w_out, w_peer_query, peer_sub_keys, seq, tm)
    return _peer(h2t, peer_expert_in, peer_expert_out, s_all, stats, v2top, x1, ada3, row1(g_post_ffn),
                 seq, tm=min(1024, seq), te=512)


def kernel(x, c, positions, w_ada, b_ada, g_pre_mix, g_post_mix, w_in, w_up_sb, w_up_sa, w_out,
           g_pre_ffn, g_post_ffn, w_peer_query, peer_sub_keys, peer_expert_in, peer_expert_out):
    bsz, seq, d = x.shape
    x2 = x.reshape(bsz * seq, d)
    for l in range(w_ada.shape[0]):
        x2 = _layer(x2, c, positions, bsz, seq, w_ada[l], b_ada[l], g_pre_mix[l], g_post_mix[l],
                    w_in[l], w_up_sb[l], w_up_sa[l], w_out[l], g_pre_ffn[l], g_post_ffn[l],
                    w_peer_query[l], peer_sub_keys[l], peer_expert_in[l], peer_expert_out[l])
    return x2.reshape(bsz, seq, d)
```

```python
import functools
import math

import numpy as np
import jax
import jax.numpy as jnp
from jax import lax
from jax.experimental import pallas as pl
from jax.experimental.pallas import tpu as pltpu

HEAD_DIM = 64
N_HEADS_SB = 8
N_HEADS_SA = 8
N_KV_SA = 2
N_IDX_HEADS = 8
IDX_DIM = 64
TOPK_MAX = 256
ROPE_THETA = 500000.0
ROT_DIM = HEAD_DIM // 4
ROT_HALF = ROT_DIM // 2
PEER_HEADS = 8
PEER_NKEYS = 128
PEER_QDIM = 256
PEER_TOPK = 16
NORM_EPS = 1e-6

LANES = 128
NEG_BIG = -1e30
VMEM_LIMIT = 56 * 1024 * 1024

F32 = jnp.float32
BF16 = jnp.bfloat16


def _cparams(sem):
    return pltpu.CompilerParams(dimension_semantics=sem, vmem_limit_bytes=VMEM_LIMIT)


def _dot(a, b):
    return jnp.dot(a, b, preferred_element_type=F32)


def _dot_nt(a, b):
    return lax.dot_general(a, b, (((1,), (1,)), ((), ())), preferred_element_type=F32)


def _split(a):
    hi = a.astype(BF16)
    lo = (a - hi.astype(F32)).astype(BF16)
    return hi, lo


def _dot3(a, b):
    ah, al = _split(a)
    bh, bl = _split(b)
    return _dot(ah, bh) + (_dot(ah, bl) + _dot(al, bh))


def _rms(x, g):
    return x * lax.rsqrt(jnp.mean(x * x, axis=-1, keepdims=True) + NORM_EPS) * g


def _ada_kernel(c_ref, w_ref, b_ref, o_ref):
    c = c_ref[...]
    s = c / (1.0 + jnp.exp(-c))
    o_ref[...] = _dot3(s, w_ref[...]) + b_ref[...]


def _ada(c, w, b):
    bsz, d = c.shape
    n_out = w.shape[1]
    return pl.pallas_call(
        _ada_kernel,
        grid=(n_out // d,),
        in_specs=[
            pl.BlockSpec((bsz, d), lambda j: (0, 0)),
            pl.BlockSpec((d, d), lambda j: (0, j)),
            pl.BlockSpec((1, d), lambda j: (0, j)),
        ],
        out_specs=pl.BlockSpec((bsz, d), lambda j: (0, j)),
        out_shape=jax.ShapeDtypeStruct((bsz, n_out), F32),
        compiler_params=_cparams(("parallel",)),
        name="ada",
    )(c, w, b.reshape(1, n_out))


def _rope_rows(x, cos_t, sin_t):
    lane = lax.broadcasted_iota(jnp.int32, (1, LANES), 1) % HEAD_DIM
    c = jnp.where(lane < ROT_DIM, cos_t, 1.0)
    s_lo = jnp.where(lane < ROT_HALF, -sin_t, 0.0)
    s_hi = jnp.where((lane >= ROT_HALF) & (lane < ROT_DIM), sin_t, 0.0)
    outs = []
    for j in range(x.shape[1] // LANES):
        xb = x[:, j * LANES:(j + 1) * LANES]
        up = pltpu.roll(xb, LANES - ROT_HALF, 1)
        dn = pltpu.roll(xb, ROT_HALF, 1)
        outs.append(xb * c + up * s_lo + dn * s_hi)
    return jnp.concatenate(outs, axis=1) if len(outs) > 1 else outs[0]


def _rope_cols(x, cos_t, sin_t):
    outs = []
    for j in range(x.shape[0] // HEAD_DIM):
        blk = x[j * HEAD_DIM:(j + 1) * HEAD_DIM]
        x1 = blk[0:ROT_HALF]
        x2 = blk[ROT_HALF:ROT_DIM]
        outs += [x1 * cos_t - x2 * sin_t, x2 * cos_t + x1 * sin_t, blk[ROT_DIM:]]
    return jnp.concatenate(outs, axis=0)


def _inproj_kernel(x_ref, posc_ref, posr_ref, shift_ref, scale_ref, g_ref, invl_ref, invc_ref,
                   wq_sb, wkt_sb, wv_sb, wq_sa, wkt_sa, wv_sa, wq_ix, wkt_ix, w_aux, w_gate,
                   q_sb_o, kt_sb_o, v_sb_o, q_sa_o, kt_sa_o, v_sa_o, q_ix_o, kt_ix_o, wix_o, sg_o):
    x = x_ref[...]
    h = _rms(x, g_ref[...]) * (1.0 + scale_ref[...]) + shift_ref[...]
    hb = h.astype(BF16)

    ang_r = posc_ref[...].astype(F32) * invl_ref[...]
    cos_r, sin_r = jnp.cos(ang_r), jnp.sin(ang_r)
    ang_c = posr_ref[...].astype(F32) * invc_ref[...]
    cos_c, sin_c = jnp.cos(ang_c), jnp.sin(ang_c)

    scale = HEAD_DIM ** -0.5
    q_sb_o[...] = (_dot(hb, wq_sb[...]) * scale).astype(BF16)
    kt_sb_o[...] = _dot_nt(wkt_sb[...], hb).astype(BF16)
    v_sb_o[...] = _dot(hb, wv_sb[...]).astype(BF16)
    q_sa_o[...] = (_rope_rows(_dot(hb, wq_sa[...]), cos_r, sin_r) * scale).astype(BF16)
    kt_sa_o[...] = _rope_cols(_dot_nt(wkt_sa[...], hb), cos_c, sin_c).astype(BF16)
    v_sa_o[...] = _dot(hb, wv_sa[...]).astype(BF16)
    q_ix_o[...] = _rope_rows(_dot(hb, wq_ix[...]), cos_r, sin_r).astype(BF16)
    kt_ix_o[...] = _rope_cols(_dot_nt(wkt_ix[...], hb), cos_c, sin_c).astype(BF16)
    wix_o[...] = _dot(hb, w_aux[...]) * (IDX_DIM ** -0.5 * N_IDX_HEADS ** -0.5)
    gate = _dot(hb, w_gate[...])
    sg_o[...] = (1.0 / (1.0 + jnp.exp(-gate))).astype(BF16)


def _inproj(x2, pos, shift_scale, g_pre, w_in, bsz, seq, tm):
    n, d = x2.shape
    w_sb = N_HEADS_SB * HEAD_DIM
    w_sa = N_HEADS_SA * HEAD_DIM
    w_kv = N_KV_SA * HEAD_DIM
    w_ixq = N_IDX_HEADS * IDX_DIM
    cuts = np.cumsum([w_sb, w_sb, w_sb, w_sa, w_kv, w_kv, w_ixq, IDX_DIM, N_IDX_HEADS, d, d])
    c = [0] + [int(v) for v in cuts]
    wb = w_in.astype(BF16)
    seg = lambda i: wb[:, c[i]:c[i + 1]]
    wq_sb, wk_sb, wv_sb, wq_sa, wk_sa, wv_sa, wq_ix, wk_ix, w_ixw = [seg(i) for i in range(9)]
    w_gate = wb[:, c[9]:c[11]]
    dup = lambda w: jnp.concatenate(
        [w[:, j * HEAD_DIM:(j + 1) * HEAD_DIM] for j in range(w.shape[1] // HEAD_DIM) for _ in (0, 1)], axis=1)
    wkt_sb = wk_sb.T
    wkt_sa = dup(wk_sa).T
    wv_sa2 = dup(wv_sa)
    wkt_ix = dup(wk_ix).T
    w_aux = jnp.concatenate([w_ixw, jnp.zeros((d, LANES - N_IDX_HEADS), BF16)], axis=1)

    inv_freq = jnp.power(ROPE_THETA, -2.0 * jnp.arange(ROT_HALF, dtype=F32) / ROT_DIM)
    inv_lane = jnp.tile(inv_freq, LANES // ROT_HALF).reshape(1, LANES)
    inv_col = jnp.broadcast_to(inv_freq[:, None], (ROT_HALF, tm))

    nt = seq // tm
    row = lambda w: pl.BlockSpec((tm, w), lambda i: (i, 0))
    colT = lambda r: pl.BlockSpec((None, r, tm), lambda i: (i // nt, 0, i % nt))
    const = lambda a: pl.BlockSpec(a.shape, lambda i: (0,) * a.ndim)
    weights = [wq_sb, wkt_sb, wv_sb, wq_sa, wkt_sa, wv_sa2, wq_ix, wkt_ix, w_aux, w_gate]
    out_shape = [
        jax.ShapeDtypeStruct((n, w_sb), BF16),
        jax.ShapeDtypeStruct((bsz, w_sb, seq), BF16),
        jax.ShapeDtypeStruct((n, w_sb), BF16),
        jax.ShapeDtypeStruct((n, w_sa), BF16),
        jax.ShapeDtypeStruct((bsz, 2 * w_kv, seq), BF16),
        jax.ShapeDtypeStruct((n, 2 * w_kv), BF16),
        jax.ShapeDtypeStruct((n, w_ixq), BF16),
        jax.ShapeDtypeStruct((bsz, 2 * IDX_DIM, seq), BF16),
        jax.ShapeDtypeStruct((n, LANES), F32),
        jax.ShapeDtypeStruct((n, 2 * d), BF16),
    ]
    out_specs = [row(w_sb), colT(w_sb), row(w_sb), row(w_sa), colT(2 * w_kv), row(2 * w_kv),
                 row(w_ixq), colT(2 * IDX_DIM), row(LANES), row(2 * d)]
    return pl.pallas_call(
        _inproj_kernel,
        grid=(n // tm,),
        in_specs=[
            row(d),
            pl.BlockSpec((tm, 1), lambda i: (i, 0)),
            pl.BlockSpec((None, 1, tm), lambda i: (i // nt, 0, i % nt)),
            pl.BlockSpec((None, 1, d), lambda i: (i // nt, 0, 0)),
            pl.BlockSpec((None, 1, d), lambda i: (i // nt, 0, 1)),
            const(g_pre), const(inv_lane), const(inv_col),
        ] + [const(w) for w in weights],
        out_specs=out_specs,
        out_shape=out_shape,
        compiler_params=_cparams(("parallel",)),
        name="inproj",
    )(x2, pos.reshape(n, 1), pos.reshape(bsz, 1, seq), shift_scale, shift_scale, g_pre,
      inv_lane, inv_col, *weights)


def _softplus(z):
    return jnp.maximum(z, 0.0) + jnp.log(1.0 + jnp.exp(-jnp.abs(z)))


def _sb_kernel(q_ref, kt_ref, v_ref, o_ref, *, tq):
    qi = pl.program_id(2)
    q2 = q_ref[...]
    lane_half = lax.broadcasted_iota(jnp.int32, (1, LANES), 1) // HEAD_DIM
    r_i = lax.broadcasted_iota(jnp.int32, (tq, tq), 0)
    c_i = lax.broadcasted_iota(jnp.int32, (tq, tq), 1)
    strict = c_i < r_i
    ck = min(tq, 2 * LANES)
    later = jnp.where(lax.broadcasted_iota(jnp.int32, (ck, ck), 0) > lax.broadcasted_iota(jnp.int32, (ck, ck), 1),
                      1.0, 0.0).astype(BF16)
    qms = [jnp.where(lane_half == hh, q2, jnp.zeros_like(q2)) for hh in (0, 1)]

    def block(kb, state, diag):
        k0 = pl.multiple_of(kb * tq, tq)
        kt = kt_ref[:, pl.ds(k0, tq)]
        v = v_ref[pl.ds(k0, tq), :]
        new = []
        for hh in (0, 1):
            carry, acc = state[2 * hh], state[2 * hh + 1]
            z = _dot(qms[hh], kt)
            sp = _softplus(z)
            lf = -sp
            if diag:
                lf = jnp.where(strict, lf, 0.0)
            chunks = [None] * (tq // ck)
            for c in reversed(range(tq // ck)):
                lf_c = lf[:, c * ck:(c + 1) * ck]
                chunks[c] = _dot(lf_c.astype(BF16), later) + carry
                carry = carry + jnp.sum(lf_c, axis=1, keepdims=True)
            between = jnp.concatenate(chunks, axis=1) if len(chunks) > 1 else chunks[0]
            a = jnp.exp((z - sp) + between)
            if diag:
                a = jnp.where(strict, a, 0.0)
            new += [carry, acc + _dot(a.astype(BF16), v)]
        return tuple(new)

    zero = (jnp.zeros((tq, 1), F32), jnp.zeros((tq, LANES), F32))
    state = block(qi, zero + zero, True)
    state = lax.fori_loop(0, qi, lambda it, st: block(qi - 1 - it, st, False), state)
    o_ref[...] = jnp.where(lane_half == 0, state[1], state[3]).astype(o_ref.dtype)


def _sb_attn(q, kt, v, bsz, seq, tq):
    n, w = q.shape
    npair = w // LANES
    nq = seq // tq
    return pl.pallas_call(
        functools.partial(_sb_kernel, tq=tq),
        grid=(bsz, npair, nq),
        in_specs=[
            pl.BlockSpec((tq, LANES), lambda b, p, i: (b * nq + i, p)),
            pl.BlockSpec((None, LANES, seq), lambda b, p, i: (b, p, 0)),
            pl.BlockSpec((seq, LANES), lambda b, p, i: (b, p)),
        ],
        out_specs=pl.BlockSpec((tq, LANES), lambda b, p, i: (b * nq + i, p)),
        out_shape=jax.ShapeDtypeStruct((n, w), BF16),
        compiler_params=_cparams(("parallel", "parallel", "arbitrary")),
        name="sb_attn",
    )(q, kt, v)


def _sort_key(s):
    bits = lax.bitcast_convert_type(s, jnp.int32)
    bits = jnp.where(bits == jnp.int32(-2 ** 31), 0, bits)
    return bits ^ ((bits >> 31) & jnp.int32(0x7FFFFFFF))


def _dsa_kernel(qix_ref, ktix_ref, wix_ref, q_ref, kt_ref, v_ref, o_ref,
                qm_scr, key_scr, bias_scr, qs_scr, m_scr, acc_scr, tau_scr, ngt_scr, *, tq, nq, k_sel):
    qi = pl.program_id(1)
    nkb = qi + 1
    lane_half = lax.broadcasted_iota(jnp.int32, (1, LANES), 1) // HEAD_DIM
    r_i = lax.broadcasted_iota(jnp.int32, (tq, tq), 0)
    c_i = lax.broadcasted_iota(jnp.int32, (tq, tq), 1)
    upto = jnp.where(r_i <= c_i, 1.0, 0.0).astype(BF16)

    for h in range(N_IDX_HEADS):
        qp = qix_ref[:, (h // 2) * LANES:(h // 2 + 1) * LANES]
        qm_scr[h] = jnp.where(lane_half == h % 2, qp, jnp.zeros_like(qp))

    def score_body(kb, _):
        k0 = pl.multiple_of(kb * tq, tq)
        kk = ktix_ref[:, pl.ds(k0, tq)]
        sc = jnp.zeros((tq, tq), F32)
        for h in range(N_IDX_HEADS):
            rel = jnp.maximum(_dot(qm_scr[h], kk), 0.0)
            sc = sc + rel * wix_ref[:, h:h + 1]
        causal = (c_i + k0) <= (r_i + qi * tq)
        sc = jnp.where(causal, sc, -jnp.inf)
        key_scr[:, pl.ds(k0, tq)] = _sort_key(sc)
        return 0

    lax.fori_loop(0, nkb, score_body, 0)

    n_grp = 4
    rg = tq // n_grp

    def search(nblk):
        def count_ge(g, cand):
            cand_b = jnp.broadcast_to(cand, (rg, LANES))
            part = jnp.zeros((rg, LANES), jnp.int32)
            for j in range(nblk * tq // LANES):
                blk = key_scr[g * rg:(g + 1) * rg, j * LANES:(j + 1) * LANES]
                part = part + jnp.where(blk >= cand_b, 1, 0)
            return jnp.sum(part, axis=1, keepdims=True)

        def bit_body(it, curs):
            bit = jnp.int32(1) << (31 - it)
            return tuple(jnp.where(count_ge(g, curs[g] + bit) >= k_sel, curs[g] + bit, curs[g])
                         for g in range(n_grp))

        lowest = jnp.full((rg, 1), -2 ** 31, jnp.int32)
        curs = lax.fori_loop(0, 32, bit_body, (lowest,) * n_grp)
        for g in range(n_grp):
            tau_scr[g * rg:(g + 1) * rg, :] = curs[g]
            ngt_scr[g * rg:(g + 1) * rg, :] = count_ge(g, curs[g] + 1)

    for q in range(nq):
        pl.when(qi == q)(functools.partial(search, q + 1))

    tau = tau_scr[...]
    need = (k_sel - ngt_scr[...]).astype(F32)

    def bias_body(kb, seen):
        k0 = pl.multiple_of(kb * tq, tq)
        key = key_scr[:, pl.ds(k0, tq)]
        tie = key == tau
        rank = _dot(jnp.where(tie, 1.0, 0.0).astype(BF16), upto) + seen
        sel = (key > tau) | (tie & (rank <= need))
        causal = (c_i + k0) <= (r_i + qi * tq)
        bias_scr[:, pl.ds(k0, tq)] = jnp.where(sel & causal, 0.0, NEG_BIG)
        return rank[:, tq - 1:tq]

    lax.fori_loop(0, nkb, bias_body, jnp.zeros((tq, 1), F32))

    gsz = N_HEADS_SA // N_KV_SA
    for g in range(N_KV_SA):
        for hh in range(gsz):
            h = g * gsz + hh
            qp = q_ref[:, (h // 2) * LANES:(h // 2 + 1) * LANES]
            qs_scr[hh * tq:(hh + 1) * tq, :] = jnp.where(lane_half == h % 2, qp, jnp.zeros_like(qp))
        m_scr[...] = jnp.full(m_scr.shape, -jnp.inf, F32)
        acc_scr[...] = jnp.zeros(acc_scr.shape, F32)

        def masked_logits(k0, w, g=g):
            kk = kt_ref[g * LANES:(g + 1) * LANES, pl.ds(k0, w)]
            logits = _dot(qs_scr[...], kk)
            return (logits.reshape(gsz, tq, w) + bias_scr[:, pl.ds(k0, w)][None]).reshape(gsz * tq, w)

        def max_step(k0, w):
            logits = masked_logits(k0, w)
            part = m_scr[...]
            for j in range(w // LANES):
                part = jnp.maximum(part, logits[:, j * LANES:(j + 1) * LANES])
            m_scr[...] = part

        def attn_step(k0, w, g=g):
            logits = masked_logits(k0, w)
            mrow = m_scr[...]
            p = jnp.concatenate([jnp.exp(logits[:, j * LANES:(j + 1) * LANES] - mrow)
                                 for j in range(w // LANES)], axis=1).astype(BF16)
            vv = jnp.concatenate([v_ref[pl.ds(k0, w), g * LANES:(g + 1) * LANES],
                                  jnp.ones((w, LANES), BF16)], axis=1)
            acc_scr[...] += _dot(p, vv)

        def sweep(step):
            def pair_body(it, _):
                step(pl.multiple_of(it * (2 * tq), 2 * tq), 2 * tq)
                return 0
            lax.fori_loop(0, nkb // 2, pair_body, 0)
            pl.when(nkb % 2 == 1)(lambda: step(pl.multiple_of((nkb - 1) * tq, tq), tq))

        sweep(max_step)
        m_scr[...] = jnp.broadcast_to(jnp.max(m_scr[...], axis=1, keepdims=True), m_scr.shape)
        sweep(attn_step)
        for hp in range(gsz // 2):
            pair = []
            for hh in (2 * hp, 2 * hp + 1):
                a = acc_scr[hh * tq:(hh + 1) * tq, :]
                pair.append(a[:, :LANES] / a[:, LANES:LANES + 1])
            col = (g * gsz // 2 + hp) * LANES
            o_ref[:, col:col + LANES] = jnp.where(lane_half == 0, pair[0], pair[1]).astype(o_ref.dtype)


def _dsa_attn(q_ix, kt_ix, w_ix, q, kt, v, bsz, seq, tq):
    n, w = q.shape
    nq = seq // tq
    k_sel = min(TOPK_MAX, seq // 4)
    gsz = N_HEADS_SA // N_KV_SA
    rowb = lambda width: pl.BlockSpec((tq, width), lambda b, i: (b * nq + i, 0))
    return pl.pallas_call(
        functools.partial(_dsa_kernel, tq=tq, nq=nq, k_sel=k_sel),
        grid=(bsz, nq),
        in_specs=[
            rowb(q_ix.shape[1]),
            pl.BlockSpec((None, kt_ix.shape[1], seq), lambda b, i: (b, 0, 0)),
            rowb(LANES),
            rowb(w),
            pl.BlockSpec((None, kt.shape[1], seq), lambda b, i: (b, 0, 0)),
            pl.BlockSpec((seq, v.shape[1]), lambda b, i: (b, 0)),
        ],
        out_specs=rowb(w),
        out_shape=jax.ShapeDtypeStruct((n, w), BF16),
        scratch_shapes=[
            pltpu.VMEM((N_IDX_HEADS, tq, LANES), BF16),
            pltpu.VMEM((tq, seq), jnp.int32),
            pltpu.VMEM((tq, seq), F32),
            pltpu.VMEM((gsz * tq, LANES), BF16),
            pltpu.VMEM((gsz * tq, LANES), F32),
            pltpu.VMEM((gsz * tq, 2 * LANES), F32),
            pltpu.VMEM((tq, 1), jnp.int32),
            pltpu.VMEM((tq, 1), jnp.int32),
        ],
        compiler_params=_cparams(("parallel", "arbitrary")),
        name="dsa_attn",
    )(q_ix, kt_ix, w_ix, q, kt, v)


def _put_row(stack, r, row):
    idx = lax.broadcasted_iota(jnp.int32, stack.shape, 0)
    return jnp.where(idx == r, row, stack)


SUBLANES = 8


def _oddeven_pairs(n):
    pairs, p = [], 1
    while p < n:
        k = p
        while k >= 1:
            for j in range(k % p, n - k, 2 * k):
                for i in range(min(k, n - j - k)):
                    if (i + j) // (2 * p) == (i + j + k) // (2 * p):
                        pairs.append((i + j, i + j + k))
            k //= 2
        p *= 2
    return pairs


def _exchange(a, b):
    if a is None:
        return b, None
    if b is None:
        return a, None
    return jnp.maximum(a, b), jnp.minimum(a, b)


def _top_desc(x, k):
    rows, tm = x.shape
    v = [x[SUBLANES * r:SUBLANES * (r + 1)] for r in range(rows // SUBLANES)] + [None] * (k - rows // SUBLANES)
    for i, j in _oddeven_pairs(k):
        v[i], v[j] = _exchange(v[i], v[j])
    shift = SUBLANES // 2
    while shift >= 1:
        rolled = [None if a is None else pltpu.roll(a, shift, 0) for a in v]
        merged = []
        for r in range(k):
            a, b = v[r], rolled[k - 1 - r]
            merged.append(b if a is None else (a if b is None else jnp.maximum(a, b)))
        v = merged
        d = k // 2
        while d >= 1:
            for r in range(k):
                if (r // d) % 2 == 0:
                    v[r], v[r + d] = _exchange(v[r], v[r + d])
            d //= 2
        shift //= 2
    sub = lax.broadcasted_iota(jnp.int32, (SUBLANES, tm), 0)
    slabs = []
    for s0 in range(0, k, SUBLANES):
        slab = v[s0]
        for r in range(1, SUBLANES):
            slab = jnp.where(sub == r, v[s0 + r], slab)
        slabs.append(slab)
    return jnp.concatenate(slabs, axis=0)


def _merge_kernel(osb_ref, osa_ref, sg_ref, x_ref, gate1_ref, shift2_ref, scale2_ref,
                  gpost_ref, gpre_ref, wup_sb, wup_sa, wout, wqt, keys_ref,
                  x1_o, h2t_o, s_o, st_o, v2_o):
    d = x_ref.shape[1]
    y_sb = _dot(osb_ref[...], wup_sb[...])
    y_sa = _dot(osa_ref[...], wup_sa[...])
    sg = sg_ref[...].astype(F32)
    merged = sg[:, :d] * y_sb + sg[:, d:] * y_sa
    y = _dot(merged.astype(BF16), wout[...])
    x1 = x_ref[...] + gate1_ref[...] * _rms(y, gpost_ref[...])
    x1_o[...] = x1
    h2 = _rms(x1, gpre_ref[...]) * (1.0 + scale2_ref[...]) + shift2_ref[...]
    h2t = h2.T.astype(BF16)
    h2t_o[...] = h2t
    qt = _dot(wqt[...], h2t)

    half = PEER_QDIM // 2
    stats = jnp.zeros(st_o.shape, F32)
    for h in range(PEER_HEADS):
        tops = []
        for p in (0, 1):
            r0 = (2 * h + p) * half
            q_hi, q_lo = _split(qt[r0:r0 + half])
            kf = keys_ref[2 * h + p]
            k_hi, k_lo = _split(kf)
            s = _dot(k_hi, q_hi) + (_dot(k_hi, q_lo) + _dot(k_lo, q_hi))
            s_o[2 * h + p] = s
            tops.append(_top_desc(s, PEER_TOPK))
        v2_o[h * PEER_TOPK:(h + 1) * PEER_TOPK, :] = tops[1]
        hk = PEER_TOPK // 2
        cand = jnp.concatenate(
            [tops[0][0:1] + tops[1]]
            + [tops[0][k1:k1 + 1] + tops[1][0:hk] for k1 in range(1, hk)]
            + [tops[0][hk:] + tops[1][0:1]], axis=0)
        best = _top_desc(cand, PEER_TOPK)
        mx = best[0:1]
        z = jnp.sum(jnp.exp(best - mx), axis=0, keepdims=True)
        stats = _put_row(stats, h, best[PEER_TOPK - 1:PEER_TOPK])
        stats = _put_row(stats, PEER_HEADS + h, mx)
        stats = _put_row(stats, 2 * PEER_HEADS + h, 1.0 / z)
    st_o[...] = stats


def _merge(o_sb, o_sa, sg, x2, ada3, g_post, g_pre, w_up_sb, w_up_sa, w_out, w_query, sub_keys,
           seq, tm):
    n, d = x2.shape
    nt = seq // tm
    nk = 2 * PEER_HEADS
    wqt = w_query.T.astype(BF16)
    keys = sub_keys.reshape(nk, PEER_NKEYS, PEER_QDIM // 2)
    wup_sb, wup_sa, wout = w_up_sb.astype(BF16), w_up_sa.astype(BF16), w_out.astype(BF16)
    row = lambda w: pl.BlockSpec((tm, w), lambda i: (i, 0))
    const = lambda a: pl.BlockSpec(a.shape, lambda i: (0,) * a.ndim)
    adab = lambda j: pl.BlockSpec((None, 1, d), lambda i: (i // nt, 0, j))
    return pl.pallas_call(
        _merge_kernel,
        grid=(n // tm,),
        in_specs=[row(o_sb.shape[1]), row(o_sa.shape[1]), row(2 * d), row(d),
                  adab(2), adab(3), adab(4), const(g_post), const(g_pre),
                  const(wup_sb), const(wup_sa), const(wout), const(wqt), const(keys)],
        out_specs=[row(d),
                   pl.BlockSpec((d, tm), lambda i: (0, i)),
                   pl.BlockSpec((nk, PEER_NKEYS, tm), lambda i: (0, 0, i)),
                   pl.BlockSpec((4 * PEER_HEADS, tm), lambda i: (0, i)),
                   pl.BlockSpec((PEER_HEADS * PEER_TOPK, tm), lambda i: (0, i))],
        out_shape=[jax.ShapeDtypeStruct((n, d), F32),
                   jax.ShapeDtypeStruct((d, n), BF16),
                   jax.ShapeDtypeStruct((nk, PEER_NKEYS, n), F32),
                   jax.ShapeDtypeStruct((4 * PEER_HEADS, n), F32),
                   jax.ShapeDtypeStruct((PEER_HEADS * PEER_TOPK, n), F32)],
        compiler_params=_cparams(("parallel",)),
        name="merge",
    )(o_sb, o_sa, sg, x2, ada3, ada3, ada3, g_post, g_pre, wup_sb, wup_sa, wout, wqt, keys)


_ERFC_P = 0.3275911
_ERFC_C = (0.254829592, -0.284496736, 1.421413741, -1.453152027, 1.061405429)


def _gelu(a):
    u = jnp.abs(a)
    t = 1.0 / (1.0 + (_ERFC_P * 2.0 ** -0.5) * u)
    c = [0.5 * v for v in _ERFC_C]
    q = t * (c[0] + t * (c[1] + t * (c[2] + t * (c[3] + t * c[4])))) * jnp.exp2((-0.5 * math.log2(math.e)) * (a * a))
    return a * jnp.where(a >= 0.0, 1.0 - q, q)


BF16_ROWS = 16


def _pair_bits(x):
    u = lax.bitcast_convert_type(x.astype(BF16).astype(F32), jnp.uint32)
    return u | (u >> 16)


def _bcast_bf16(row_u32):
    tm = row_u32.shape[1]
    return pltpu.bitcast(jnp.broadcast_to(row_u32, (BF16_ROWS // 2, tm)), BF16)


def _peer_kernel(h2t_ref, win_ref, woutt_ref, s_ref, st_ref, v2_ref, x1_ref, gate2_ref, gpost_ref, o_ref,
                 acc_scr, act_scr, cnt_scr, e1_scr, rk_scr, e2_scr, *, te):
    k = pl.program_id(1)
    nk = PEER_NKEYS
    tm = h2t_ref.shape[1]

    @pl.when(k == 0)
    def _():
        acc_scr[...] = jnp.zeros(acc_scr.shape, F32)
        for h in range(PEER_HEADS):
            s1 = s_ref[2 * h]
            s2 = s_ref[2 * h + 1]
            tau = st_ref[h:h + 1, :]
            inv_z = st_ref[2 * PEER_HEADS + h:2 * PEER_HEADS + h + 1, :]
            cnt = jnp.zeros(s1.shape, F32)
            rank = jnp.zeros(s2.shape, F32)
            for k2 in range(PEER_TOPK):
                v = v2_ref[h * PEER_TOPK + k2:h * PEER_TOPK + k2 + 1, :]
                cnt = jnp.where(s1 + v >= tau, k2 + 1.0, cnt)
                rank = jnp.where(v > s2, k2 + 1.0, rank)
            cnt_scr[h] = _pair_bits(cnt)
            e1_scr[h] = _pair_bits(jnp.exp(s1 - jnp.max(s1, axis=0, keepdims=True)))
            rk_scr[h] = rank.astype(BF16)
            e2_scr[h] = (jnp.exp(s2 - v2_ref[h * PEER_TOPK:h * PEER_TOPK + 1, :]) * inv_z).astype(BF16)

    a = _dot(win_ref[...], h2t_ref[...])
    gel = _gelu(a)
    grp = nk // BF16_ROWS
    for ii in range(te // nk):
        i = k * (te // nk) + ii
        gate = jnp.zeros((grp, BF16_ROWS, tm), BF16)
        for h in range(PEER_HEADS):
            cnt = _bcast_bf16(cnt_scr[h, pl.ds(i, 1), :])[None]
            e1 = _bcast_bf16(e1_scr[h, pl.ds(i, 1), :])[None]
            rk = rk_scr[h].reshape(grp, BF16_ROWS, tm)
            e2 = e2_scr[h].reshape(grp, BF16_ROWS, tm)
            gate = gate + jnp.where(rk < cnt, e2, jnp.zeros_like(e2)) * e1
        act_scr[ii * nk:(ii + 1) * nk, :] = gate.reshape(nk, tm) * gel[ii * nk:(ii + 1) * nk].astype(BF16)
    acc_scr[...] += _dot(woutt_ref[...], act_scr[...])

    @pl.when(k == pl.num_programs(1) - 1)
    def _():
        o_ref[...] = x1_ref[...] + gate2_ref[...] * _rms(acc_scr[...].T, gpost_ref[...])


def _peer(h2t, w_in_e, w_out_e, s_all, stats, v2top, x1, ada3, g_post, seq, tm, te):
    d, n = h2t.shape
    nt = seq // tm
    n_exp = w_in_e.shape[0]
    win = w_in_e.astype(BF16)
    woutt = w_out_e.astype(BF16).reshape(n_exp // te, te, d).transpose(0, 2, 1)
    once = dict(pipeline_mode=pl.Buffered(1))
    tab_u = pltpu.VMEM((PEER_HEADS, PEER_NKEYS, tm), jnp.uint32)
    tab_b = pltpu.VMEM((PEER_HEADS, PEER_NKEYS, tm), BF16)
    return pl.pallas_call(
        functools.partial(_peer_kernel, te=te),
        grid=(n // tm, n_exp // te),
        in_specs=[
            pl.BlockSpec((d, tm), lambda i, k: (0, i), **once),
            pl.BlockSpec((te, d), lambda i, k: (k, 0)),
            pl.BlockSpec((None, d, te), lambda i, k: (k, 0, 0)),
            pl.BlockSpec((s_all.shape[0], PEER_NKEYS, tm), lambda i, k: (0, 0, i), **once),
            pl.BlockSpec((stats.shape[0], tm), lambda i, k: (0, i), **once),
            pl.BlockSpec((v2top.shape[0], tm), lambda i, k: (0, i), **once),
            pl.BlockSpec((tm, d), lambda i, k: (i, 0), **once),
            pl.BlockSpec((None, 1, d), lambda i, k: (i // nt, 0, 5)),
            pl.BlockSpec((1, d), lambda i, k: (0, 0)),
        ],
        out_specs=pl.BlockSpec((tm, d), lambda i, k: (i, 0)),
        out_shape=jax.ShapeDtypeStruct((n, d), F32),
        scratch_shapes=[pltpu.VMEM((d, tm), F32), pltpu.VMEM((te, tm), BF16), tab_u, tab_u, tab_b, tab_b],
        compiler_params=_cparams(("parallel", "arbitrary")),
        name="peer",
    )(h2t, win, woutt, s_all, stats, v2top, x1, ada3, g_post)


def _layer(x2, c, pos, bsz, seq, w_ada, b_ada, g_pre_mix, g_post_mix, w_in, w_up_sb, w_up_sa, w_out,
           g_pre_ffn, g_post_ffn, w_peer_query, peer_sub_keys, peer_expert_in, peer_expert_out):
    n, d = x2.shape
    tq = 256 if seq % 256 == 0 else 128
    tm = 256
    row1 = lambda g: g.reshape(1, d)
    ada3 = _ada(c, w_ada, b_ada).reshape(bsz, 1, 6 * d)
    (q_sb, kt_sb, v_sb, q_sa, kt_sa, v_sa, q_ix, kt_ix, w_ix, sg) = _inproj(
        x2, pos, ada3, row1(g_pre_mix), w_in, bsz, seq, 512 if seq % 512 == 0 else tm)
    o_sb = _sb_attn(q_sb, kt_sb, v_sb, bsz, seq, 512 if seq % 512 == 0 else tq)
    o_sa = _dsa_attn(q_ix, kt_ix, w_ix, q_sa, kt_sa, v_sa, bsz, seq, tq)
    x1, h2t, s_all, stats, v2top = _merge(o_sb, o_sa, sg, x2, ada3, row1(g_post_mix), row1(g_pre_ffn),
                                          w_up_sb, w_up_sa, w_out, w_peer_query, peer_sub_keys, seq, tm)
    return _peer(h2t, peer_expert_in, peer_expert_out, s_all, stats, v2top, x1, ada3, row1(g_post_ffn),
                 seq, tm=min(1024, seq), te=512)


def kernel(x, c, positions, w_ada, b_ada, g_pre_mix, g_post_mix, w_in, w_up_sb, w_up_sa, w_out,
           g_pre_ffn, g_post_ffn, w_peer_query, peer_sub_keys, peer_expert_in, peer_expert_out):
    bsz, seq, d = x.shape
    x2 = x.reshape(bsz * seq, d)
    for l in range(w_ada.shape[0]):
        x2 = _layer(x2, c, positions, bsz, seq, w_ada[l], b_ada[l], g_pre_mix[l], g_post_mix[l],
                    w_in[l], w_up_sb[l], w_up_sa[l], w_out[l], g_pre_ffn[l], g_post_ffn[l],
                    w_peer_query[l], peer_sub_keys[l], peer_expert_in[l], peer_expert_out[l])
    return x2.reshape(bsz, seq, d)
```

```python
import functools
import math

import numpy as np
import jax
import jax.numpy as jnp
from jax import lax
from jax.experimental import pallas as pl
from jax.experimental.pallas import tpu as pltpu

HEAD_DIM = 64
N_HEADS_SB = 8
N_HEADS_SA = 8
N_KV_SA = 2
N_IDX_HEADS = 8
IDX_DIM = 64
TOPK_MAX = 256
ROPE_THETA = 500000.0
ROT_DIM = HEAD_DIM // 4
ROT_HALF = ROT_DIM // 2
PEER_HEADS = 8
PEER_NKEYS = 128
PEER_QDIM = 256
PEER_TOPK = 16
NORM_EPS = 1e-6

LANES = 128
NEG_BIG = -1e30
VMEM_LIMIT = 56 * 1024 * 1024

F32 = jnp.float32
BF16 = jnp.bfloat16


def _cparams(sem):
    return pltpu.CompilerParams(dimension_semantics=sem, vmem_limit_bytes=VMEM_LIMIT)


def _dot(a, b):
    return jnp.dot(a, b, preferred_element_type=F32)


def _dot_nt(a, b):
    return lax.dot_general(a, b, (((1,), (1,)), ((), ())), preferred_element_type=F32)


def _split(a):
    hi = a.astype(BF16)
    lo = (a - hi.astype(F32)).astype(BF16)
    return hi, lo


def _dot3(a, b):
    ah, al = _split(a)
    bh, bl = _split(b)
    return _dot(ah, bh) + (_dot(ah, bl) + _dot(al, bh))


def _rms(x, g):
    return x * lax.rsqrt(jnp.mean(x * x, axis=-1, keepdims=True) + NORM_EPS) * g


def _ada_kernel(c_ref, w_ref, b_ref, o_ref):
    c = c_ref[...]
    s = c / (1.0 + jnp.exp(-c))
    o_ref[...] = _dot3(s, w_ref[...]) + b_ref[...]


def _ada(c, w, b):
    bsz, d = c.shape
    n_out = w.shape[1]
    return pl.pallas_call(
        _ada_kernel,
        grid=(n_out // d,),
        in_specs=[
            pl.BlockSpec((bsz, d), lambda j: (0, 0)),
            pl.BlockSpec((d, d), lambda j: (0, j)),
            pl.BlockSpec((1, d), lambda j: (0, j)),
        ],
        out_specs=pl.BlockSpec((bsz, d), lambda j: (0, j)),
        out_shape=jax.ShapeDtypeStruct((bsz, n_out), F32),
        compiler_params=_cparams(("parallel",)),
        name="ada",
    )(c, w, b.reshape(1, n_out))


def _rope_rows(x, cos_t, sin_t):
    lane = lax.broadcasted_iota(jnp.int32, (1, LANES), 1) % HEAD_DIM
    c = jnp.where(lane < ROT_DIM, cos_t, 1.0)
    s_lo = jnp.where(lane < ROT_HALF, -sin_t, 0.0)
    s_hi = jnp.where((lane >= ROT_HALF) & (lane < ROT_DIM), sin_t, 0.0)
    outs = []
    for j in range(x.shape[1] // LANES):
        xb = x[:, j * LANES:(j + 1) * LANES]
        up = pltpu.roll(xb, LANES - ROT_HALF, 1)
        dn = pltpu.roll(xb, ROT_HALF, 1)
        outs.append(xb * c + up * s_lo + dn * s_hi)
    return jnp.concatenate(outs, axis=1) if len(outs) > 1 else outs[0]


def _rope_cols(x, cos_t, sin_t):
    outs = []
    for j in range(x.shape[0] // HEAD_DIM):
        blk = x[j * HEAD_DIM:(j + 1) * HEAD_DIM]
        x1 = blk[0:ROT_HALF]
        x2 = blk[ROT_HALF:ROT_DIM]
        outs += [x1 * cos_t - x2 * sin_t, x2 * cos_t + x1 * sin_t, blk[ROT_DIM:]]
    return jnp.concatenate(outs, axis=0)


def _inproj_kernel(x_ref, posc_ref, posr_ref, shift_ref, scale_ref, g_ref, invl_ref, invc_ref,
                   wq_sb, wkt_sb, wv_sb, wq_sa, wkt_sa, wv_sa, wq_ix, wkt_ix, w_aux, w_gate,
                   q_sb_o, kt_sb_o, v_sb_o, q_sa_o, kt_sa_o, v_sa_o, q_ix_o, kt_ix_o, wix_o, sg_o):
    x = x_ref[...]
    h = _rms(x, g_ref[...]) * (1.0 + scale_ref[...]) + shift_ref[...]
    hb = h.astype(BF16)

    ang_r = posc_ref[...].astype(F32) * invl_ref[...]
    cos_r, sin_r = jnp.cos(ang_r), jnp.sin(ang_r)
    ang_c = posr_ref[...].astype(F32) * invc_ref[...]
    cos_c, sin_c = jnp.cos(ang_c), jnp.sin(ang_c)

    scale = HEAD_DIM ** -0.5
    q_sb_o[...] = (_dot(hb, wq_sb[...]) * scale).astype(BF16)
    kt_sb_o[...] = _dot_nt(wkt_sb[...], hb).astype(BF16)
    v_sb_o[...] = _dot(hb, wv_sb[...]).astype(BF16)
    q_sa_o[...] = (_rope_rows(_dot(hb, wq_sa[...]), cos_r, sin_r) * scale).astype(BF16)
    kt_sa_o[...] = _rope_cols(_dot_nt(wkt_sa[...], hb), cos_c, sin_c).astype(BF16)
    v_sa_o[...] = _dot(hb, wv_sa[...]).astype(BF16)
    q_ix_o[...] = _rope_rows(_dot(hb, wq_ix[...]), cos_r, sin_r).astype(BF16)
    kt_ix_o[...] = _rope_cols(_dot_nt(wkt_ix[...], hb), cos_c, sin_c).astype(BF16)
    wix_o[...] = _dot(hb, w_aux[...]) * (IDX_DIM ** -0.5 * N_IDX_HEADS ** -0.5)
    gate = _dot(hb, w_gate[...])
    sg_o[...] = (1.0 / (1.0 + jnp.exp(-gate))).astype(BF16)


def _inproj(x2, pos, shift_scale, g_pre, w_in, bsz, seq, tm):
    n, d = x2.shape
    w_sb = N_HEADS_SB * HEAD_DIM
    w_sa = N_HEADS_SA * HEAD_DIM
    w_kv = N_KV_SA * HEAD_DIM
    w_ixq = N_IDX_HEADS * IDX_DIM
    cuts = np.cumsum([w_sb, w_sb, w_sb, w_sa, w_kv, w_kv, w_ixq, IDX_DIM, N_IDX_HEADS, d, d])
    c = [0] + [int(v) for v in cuts]
    wb = w_in.astype(BF16)
    seg = lambda i: wb[:, c[i]:c[i + 1]]
    wq_sb, wk_sb, wv_sb, wq_sa, wk_sa, wv_sa, wq_ix, wk_ix, w_ixw = [seg(i) for i in range(9)]
    w_gate = wb[:, c[9]:c[11]]
    dup = lambda w: jnp.concatenate(
        [w[:, j * HEAD_DIM:(j + 1) * HEAD_DIM] for j in range(w.shape[1] // HEAD_DIM) for _ in (0, 1)], axis=1)
    wkt_sb = wk_sb.T
    wkt_sa = dup(wk_sa).T
    wv_sa2 = dup(wv_sa)
    wkt_ix = dup(wk_ix).T
    w_aux = jnp.concatenate([w_ixw, jnp.zeros((d, LANES - N_IDX_HEADS), BF16)], axis=1)

    inv_freq = jnp.power(ROPE_THETA, -2.0 * jnp.arange(ROT_HALF, dtype=F32) / ROT_DIM)
    inv_lane = jnp.tile(inv_freq, LANES // ROT_HALF).reshape(1, LANES)
    inv_col = jnp.broadcast_to(inv_freq[:, None], (ROT_HALF, tm))

    nt = seq // tm
    row = lambda w: pl.BlockSpec((tm, w), lambda i: (i, 0))
    colT = lambda r: pl.BlockSpec((None, r, tm), lambda i: (i // nt, 0, i % nt))
    const = lambda a: pl.BlockSpec(a.shape, lambda i: (0,) * a.ndim)
    weights = [wq_sb, wkt_sb, wv_sb, wq_sa, wkt_sa, wv_sa2, wq_ix, wkt_ix, w_aux, w_gate]
    out_shape = [
        jax.ShapeDtypeStruct((n, w_sb), BF16),
        jax.ShapeDtypeStruct((bsz, w_sb, seq), BF16),
        jax.ShapeDtypeStruct((n, w_sb), BF16),
        jax.ShapeDtypeStruct((n, w_sa), BF16),
        jax.ShapeDtypeStruct((bsz, 2 * w_kv, seq), BF16),
        jax.ShapeDtypeStruct((n, 2 * w_kv), BF16),
        jax.ShapeDtypeStruct((n, w_ixq), BF16),
        jax.ShapeDtypeStruct((bsz, 2 * IDX_DIM, seq), BF16),
        jax.ShapeDtypeStruct((n, LANES), F32),
        jax.ShapeDtypeStruct((n, 2 * d), BF16),
    ]
    out_specs = [row(w_sb), colT(w_sb), row(w_sb), row(w_sa), colT(2 * w_kv), row(2 * w_kv),
                 row(w_ixq), colT(2 * IDX_DIM), row(LANES), row(2 * d)]
    return pl.pallas_call(
        _inproj_kernel,
        grid=(n // tm,),
        in_specs=[
            row(d),
            pl.BlockSpec((tm, 1), lambda i: (i, 0)),
            pl.BlockSpec((None, 1, tm), lambda i: (i // nt, 0, i % nt)),
            pl.BlockSpec((None, 1, d), lambda i: (i // nt, 0, 0)),
            pl.BlockSpec((None, 1, d), lambda i: (i // nt, 0, 1)),
            const(g_pre), const(inv_lane), const(inv_col),
        ] + [const(w) for w in weights],
        out_specs=out_specs,
        out_shape=out_shape,
        compiler_params=_cparams(("parallel",)),
        name="inproj",
    )(x2, pos.reshape(n, 1), pos.reshape(bsz, 1, seq), shift_scale, shift_scale, g_pre,
      inv_lane, inv_col, *weights)


def _softplus(z):
    return jnp.maximum(z, 0.0) + jnp.log(1.0 + jnp.exp(-jnp.abs(z)))


def _sb_kernel(q_ref, kt_ref, v_ref, o_ref, *, tq):
    qi = pl.program_id(2)
    q2 = q_ref[...]
    lane_half = lax.broadcasted_iota(jnp.int32, (1, LANES), 1) // HEAD_DIM
    r_i = lax.broadcasted_iota(jnp.int32, (tq, tq), 0)
    c_i = lax.broadcasted_iota(jnp.int32, (tq, tq), 1)
    strict = c_i < r_i
    ck = min(tq, 2 * LANES)
    later = jnp.where(lax.broadcasted_iota(jnp.int32, (ck, ck), 0) > lax.broadcasted_iota(jnp.int32, (ck, ck), 1),
                      1.0, 0.0).astype(BF16)
    qms = [jnp.where(lane_half == hh, q2, jnp.zeros_like(q2)) for hh in (0, 1)]

    def block(kb, state, diag):
        k0 = pl.multiple_of(kb * tq, tq)
        kt = kt_ref[:, pl.ds(k0, tq)]
        v = v_ref[pl.ds(k0, tq), :]
        new = []
        for hh in (0, 1):
            carry, acc = state[2 * hh], state[2 * hh + 1]
            z = _dot(qms[hh], kt)
            sp = _softplus(z)
            lf = -sp
            if diag:
                lf = jnp.where(strict, lf, 0.0)
            chunks = [None] * (tq // ck)
            for c in reversed(range(tq // ck)):
                lf_c = lf[:, c * ck:(c + 1) * ck]
                chunks[c] = _dot(lf_c.astype(BF16), later) + carry
                carry = carry + jnp.sum(lf_c, axis=1, keepdims=True)
            between = jnp.concatenate(chunks, axis=1) if len(chunks) > 1 else chunks[0]
            a = jnp.exp((z - sp) + between)
            if diag:
                a = jnp.where(strict, a, 0.0)
            new += [carry, acc + _dot(a.astype(BF16), v)]
        return tuple(new)

    zero = (jnp.zeros((tq, 1), F32), jnp.zeros((tq, LANES), F32))
    state = block(qi, zero + zero, True)
    state = lax.fori_loop(0, qi, lambda it, st: block(qi - 1 - it, st, False), state)
    o_ref[...] = jnp.where(lane_half == 0, state[1], state[3]).astype(o_ref.dtype)


def _sb_attn(q, kt, v, bsz, seq, tq):
    n, w = q.shape
    npair = w // LANES
    nq = seq // tq
    return pl.pallas_call(
        functools.partial(_sb_kernel, tq=tq),
        grid=(bsz, npair, nq),
        in_specs=[
            pl.BlockSpec((tq, LANES), lambda b, p, i: (b * nq + i, p)),
            pl.BlockSpec((None, LANES, seq), lambda b, p, i: (b, p, 0)),
            pl.BlockSpec((seq, LANES), lambda b, p, i: (b, p)),
        ],
        out_specs=pl.BlockSpec((tq, LANES), lambda b, p, i: (b * nq + i, p)),
        out_shape=jax.ShapeDtypeStruct((n, w), BF16),
        compiler_params=_cparams(("parallel", "parallel", "arbitrary")),
        name="sb_attn",
    )(q, kt, v)


def _sort_key(s):
    bits = lax.bitcast_convert_type(s, jnp.int32)
    bits = jnp.where(bits == jnp.int32(-2 ** 31), 0, bits)
    return bits ^ ((bits >> 31) & jnp.int32(0x7FFFFFFF))


def _dsa_kernel(qix_ref, ktix_ref, wix_ref, q_ref, kt_ref, v_ref, o_ref,
                qm_scr, key_scr, bias_scr, qs_scr, m_scr, acc_scr, tau_scr, ngt_scr, *, tq, nq, k_sel):
    qi = pl.program_id(1)
    nkb = qi + 1
    lane_half = lax.broadcasted_iota(jnp.int32, (1, LANES), 1) // HEAD_DIM
    r_i = lax.broadcasted_iota(jnp.int32, (tq, tq), 0)
    c_i = lax.broadcasted_iota(jnp.int32, (tq, tq), 1)
    upto = jnp.where(r_i <= c_i, 1.0, 0.0).astype(BF16)

    for h in range(N_IDX_HEADS):
        qp = qix_ref[:, (h // 2) * LANES:(h // 2 + 1) * LANES]
        qm_scr[h] = jnp.where(lane_half == h % 2, qp, jnp.zeros_like(qp))

    def score_body(kb, _):
        k0 = pl.multiple_of(kb * tq, tq)
        kk = ktix_ref[:, pl.ds(k0, tq)]
        sc = jnp.zeros((tq, tq), F32)
        for h in range(N_IDX_HEADS):
            rel = jnp.maximum(_dot(qm_scr[h], kk), 0.0)
            sc = sc + rel * wix_ref[:, h:h + 1]
        causal = (c_i + k0) <= (r_i + qi * tq)
        sc = jnp.where(causal, sc, -jnp.inf)
        key_scr[:, pl.ds(k0, tq)] = _sort_key(sc)
        return 0

    lax.fori_loop(0, nkb, score_body, 0)

    n_grp = 4
    rg = tq // n_grp
    assert k_sel <= 2 * LANES

    def search(nblk):
        def count_ge(g, cand):
            cand_b = jnp.broadcast_to(cand, (rg, LANES))
            part = jnp.zeros((rg, LANES), jnp.int32)
            for j in range(nblk * tq // LANES):
                blk = key_scr[g * rg:(g + 1) * rg, j * LANES:(j + 1) * LANES]
                part = part + jnp.where(blk >= cand_b, 1, 0)
            return jnp.sum(part, axis=1, keepdims=True)

        int_min = jnp.full((rg, LANES), -2 ** 31, jnp.int32)
        los, his = [], []
        for g in range(n_grp):
            m1, m2 = int_min, int_min
            for j in range(nblk * tq // LANES):
                blk = key_scr[g * rg:(g + 1) * rg, j * LANES:(j + 1) * LANES]
                m2 = jnp.maximum(m2, jnp.minimum(m1, blk))
                m1 = jnp.maximum(m1, blk)
            los.append(jnp.min(m2, axis=1, keepdims=True))
            his.append(jnp.max(m1, axis=1, keepdims=True) + 1)

        halvings = [32 - lax.clz(his[g] - los[g] - 1) for g in range(n_grp)]
        steps = jnp.max(functools.reduce(jnp.maximum, halvings))

        def halve(_, st):
            new_lo, new_hi = [], []
            for g in range(n_grp):
                lo, hi = st[0][g], st[1][g]
                mid = lo + lax.shift_right_logical(hi - lo, 1)
                ok = count_ge(g, mid) >= k_sel
                new_lo.append(jnp.where(ok, mid, lo))
                new_hi.append(jnp.where(ok, hi, mid))
            return tuple(new_lo), tuple(new_hi)

        curs, _ = lax.fori_loop(0, steps, halve, (tuple(los), tuple(his)))
        for g in range(n_grp):
            tau_scr[g * rg:(g + 1) * rg, :] = curs[g]
            ngt_scr[g * rg:(g + 1) * rg, :] = count_ge(g, curs[g] + 1)

    for q in range(nq):
        pl.when(qi == q)(functools.partial(search, q + 1))

    tau = tau_scr[...]
    need = (k_sel - ngt_scr[...]).astype(F32)

    def bias_body(kb, seen):
        k0 = pl.multiple_of(kb * tq, tq)
        key = key_scr[:, pl.ds(k0, tq)]
        tie = key == tau
        rank = _dot(jnp.where(tie, 1.0, 0.0).astype(BF16), upto) + seen
        sel = (key > tau) | (tie & (rank <= need))
        causal = (c_i + k0) <= (r_i + qi * tq)
        bias_scr[:, pl.ds(k0, tq)] = jnp.where(sel & causal, 0.0, NEG_BIG)
        return rank[:, tq - 1:tq]

    lax.fori_loop(0, nkb, bias_body, jnp.zeros((tq, 1), F32))

    gsz = N_HEADS_SA // N_KV_SA
    for g in range(N_KV_SA):
        for hh in range(gsz):
            h = g * gsz + hh
            qp = q_ref[:, (h // 2) * LANES:(h // 2 + 1) * LANES]
            qs_scr[hh * tq:(hh + 1) * tq, :] = jnp.where(lane_half == h % 2, qp, jnp.zeros_like(qp))
        m_scr[...] = jnp.full(m_scr.shape, -jnp.inf, F32)
        acc_scr[...] = jnp.zeros(acc_scr.shape, F32)

        def masked_logits(k0, w, g=g):
            kk = kt_ref[g * LANES:(g + 1) * LANES, pl.ds(k0, w)]
            logits = _dot(qs_scr[...], kk)
            return (logits.reshape(gsz, tq, w) + bias_scr[:, pl.ds(k0, w)][None]).reshape(gsz * tq, w)

        def max_step(k0, w):
            logits = masked_logits(k0, w)
            part = m_scr[...]
            for j in range(w // LANES):
                part = jnp.maximum(part, logits[:, j * LANES:(j + 1) * LANES])
            m_scr[...] = part

        def attn_step(k0, w, g=g):
            logits = masked_logits(k0, w)
            mrow = m_scr[...]
            p = jnp.concatenate([jnp.exp(logits[:, j * LANES:(j + 1) * LANES] - mrow)
                                 for j in range(w // LANES)], axis=1).astype(BF16)
            vv = jnp.concatenate([v_ref[pl.ds(k0, w), g * LANES:(g + 1) * LANES],
                                  jnp.ones((w, LANES), BF16)], axis=1)
            acc_scr[...] += _dot(p, vv)

        def sweep(step):
            def pair_body(it, _):
                step(pl.multiple_of(it * (2 * tq), 2 * tq), 2 * tq)
                return 0
            lax.fori_loop(0, nkb // 2, pair_body, 0)
            pl.when(nkb % 2 == 1)(lambda: step(pl.multiple_of((nkb - 1) * tq, tq), tq))

        sweep(max_step)
        m_scr[...] = jnp.broadcast_to(jnp.max(m_scr[...], axis=1, keepdims=True), m_scr.shape)
        sweep(attn_step)
        for hp in range(gsz // 2):
            pair = []
            for hh in (2 * hp, 2 * hp + 1):
                a = acc_scr[hh * tq:(hh + 1) * tq, :]
                pair.append(a[:, :LANES] / a[:, LANES:LANES + 1])
            col = (g * gsz // 2 + hp) * LANES
            o_ref[:, col:col + LANES] = jnp.where(lane_half == 0, pair[0], pair[1]).astype(o_ref.dtype)


def _dsa_attn(q_ix, kt_ix, w_ix, q, kt, v, bsz, seq, tq):
    n, w = q.shape
    nq = seq // tq
    k_sel = min(TOPK_MAX, seq // 4)
    gsz = N_HEADS_SA // N_KV_SA
    rowb = lambda width: pl.BlockSpec((tq, width), lambda b, i: (b * nq + i, 0))
    return pl.pallas_call(
        functools.partial(_dsa_kernel, tq=tq, nq=nq, k_sel=k_sel),
        grid=(bsz, nq),
        in_specs=[
            rowb(q_ix.shape[1]),
            pl.BlockSpec((None, kt_ix.shape[1], seq), lambda b, i: (b, 0, 0)),
            rowb(LANES),
            rowb(w),
            pl.BlockSpec((None, kt.shape[1], seq), lambda b, i: (b, 0, 0)),
            pl.BlockSpec((seq, v.shape[1]), lambda b, i: (b, 0)),
        ],
        out_specs=rowb(w),
        out_shape=jax.ShapeDtypeStruct((n, w), BF16),
        scratch_shapes=[
            pltpu.VMEM((N_IDX_HEADS, tq, LANES), BF16),
            pltpu.VMEM((tq, seq), jnp.int32),
            pltpu.VMEM((tq, seq), F32),
            pltpu.VMEM((gsz * tq, LANES), BF16),
            pltpu.VMEM((gsz * tq, LANES), F32),
            pltpu.VMEM((gsz * tq, 2 * LANES), F32),
            pltpu.VMEM((tq, 1), jnp.int32),
            pltpu.VMEM((tq, 1), jnp.int32),
        ],
        compiler_params=_cparams(("parallel", "arbitrary")),
        name="dsa_attn",
    )(q_ix, kt_ix, w_ix, q, kt, v)


def _put_row(stack, r, row):
    idx = lax.broadcasted_iota(jnp.int32, stack.shape, 0)
    return jnp.where(idx == r, row, stack)


SUBLANES = 8


def _oddeven_pairs(n):
    pairs, p = [], 1
    while p < n:
        k = p
        while k >= 1:
            for j in range(k % p, n - k, 2 * k):
                for i in range(min(k, n - j - k)):
                    if (i + j) // (2 * p) == (i + j + k) // (2 * p):
                        pairs.append((i + j, i + j + k))
            k //= 2
        p *= 2
    return pairs


def _exchange(a, b):
    if a is None:
        return b, None
    if b is None:
        return a, None
    return jnp.maximum(a, b), jnp.minimum(a, b)


def _top_desc(x, k):
    rows, tm = x.shape
    v = [x[SUBLANES * r:SUBLANES * (r + 1)] for r in range(rows // SUBLANES)] + [None] * (k - rows // SUBLANES)
    for i, j in _oddeven_pairs(k):
        v[i], v[j] = _exchange(v[i], v[j])
    shift = SUBLANES // 2
    while shift >= 1:
        rolled = [None if a is None else pltpu.roll(a, shift, 0) for a in v]
        merged = []
        for r in range(k):
            a, b = v[r], rolled[k - 1 - r]
            merged.append(b if a is None else (a if b is None else jnp.maximum(a, b)))
        v = merged
        d = k // 2
        while d >= 1:
            for r in range(k):
                if (r // d) % 2 == 0:
                    v[r], v[r + d] = _exchange(v[r], v[r + d])
            d //= 2
        shift //= 2
    sub = lax.broadcasted_iota(jnp.int32, (SUBLANES, tm), 0)
    slabs = []
    for s0 in range(0, k, SUBLANES):
        slab = v[s0]
        for r in range(1, SUBLANES):
            slab = jnp.where(sub == r, v[s0 + r], slab)
        slabs.append(slab)
    return jnp.concatenate(slabs, axis=0)


def _merge_kernel(osb_ref, osa_ref, sg_ref, x_ref, gate1_ref, shift2_ref, scale2_ref,
                  gpost_ref, gpre_ref, wup_sb, wup_sa, wout, wqt, keys_ref,
                  x1_o, h2t_o, s_o, st_o, v2_o):
    d = x_ref.shape[1]
    y_sb = _dot(osb_ref[...], wup_sb[...])
    y_sa = _dot(osa_ref[...], wup_sa[...])
    sg = sg_ref[...].astype(F32)
    merged = sg[:, :d] * y_sb + sg[:, d:] * y_sa
    y = _dot(merged.astype(BF16), wout[...])
    x1 = x_ref[...] + gate1_ref[...] * _rms(y, gpost_ref[...])
    x1_o[...] = x1
    h2 = _rms(x1, gpre_ref[...]) * (1.0 + scale2_ref[...]) + shift2_ref[...]
    h2t = h2.T.astype(BF16)
    h2t_o[...] = h2t
    qt = _dot(wqt[...], h2t)

    half = PEER_QDIM // 2
    stats = jnp.zeros(st_o.shape, F32)
    for h in range(PEER_HEADS):
        tops = []
        for p in (0, 1):
            r0 = (2 * h + p) * half
            q_hi, q_lo = _split(qt[r0:r0 + half])
            kf = keys_ref[2 * h + p]
            k_hi, k_lo = _split(kf)
            s = _dot(k_hi, q_hi) + (_dot(k_hi, q_lo) + _dot(k_lo, q_hi))
            s_o[2 * h + p] = s
            tops.append(_top_desc(s, PEER_TOPK))
        v2_o[h * PEER_TOPK:(h + 1) * PEER_TOPK, :] = tops[1]
        hk = PEER_TOPK // 2
        cand = jnp.concatenate(
            [tops[0][0:1] + tops[1]]
            + [tops[0][k1:k1 + 1] + tops[1][0:hk] for k1 in range(1, hk)]
            + [tops[0][hk:] + tops[1][0:1]], axis=0)
        best = _top_desc(cand, PEER_TOPK)
        mx = best[0:1]
        z = jnp.sum(jnp.exp(best - mx), axis=0, keepdims=True)
        stats = _put_row(stats, h, best[PEER_TOPK - 1:PEER_TOPK])
        stats = _put_row(stats, PEER_HEADS + h, mx)
        stats = _put_row(stats, 2 * PEER_HEADS + h, 1.0 / z)
    st_o[...] = stats


def _merge(o_sb, o_sa, sg, x2, ada3, g_post, g_pre, w_up_sb, w_up_sa, w_out, w_query, sub_keys,
           seq, tm):
    n, d = x2.shape
    nt = seq // tm
    nk = 2 * PEER_HEADS
    wqt = w_query.T.astype(BF16)
    keys = sub_keys.reshape(nk, PEER_NKEYS, PEER_QDIM // 2)
    wup_sb, wup_sa, wout = w_up_sb.astype(BF16), w_up_sa.astype(BF16), w_out.astype(BF16)
    row = lambda w: pl.BlockSpec((tm, w), lambda i: (i, 0))
    const = lambda a: pl.BlockSpec(a.shape, lambda i: (0,) * a.ndim)
    adab = lambda j: pl.BlockSpec((None, 1, d), lambda i: (i // nt, 0, j))
    return pl.pallas_call(
        _merge_kernel,
        grid=(n // tm,),
        in_specs=[row(o_sb.shape[1]), row(o_sa.shape[1]), row(2 * d), row(d),
                  adab(2), adab(3), adab(4), const(g_post), const(g_pre),
                  const(wup_sb), const(wup_sa), const(wout), const(wqt), const(keys)],
        out_specs=[row(d),
                   pl.BlockSpec((d, tm), lambda i: (0, i)),
                   pl.BlockSpec((nk, PEER_NKEYS, tm), lambda i: (0, 0, i)),
                   pl.BlockSpec((4 * PEER_HEADS, tm), lambda i: (0, i)),
                   pl.BlockSpec((PEER_HEADS * PEER_TOPK, tm), lambda i: (0, i))],
        out_shape=[jax.ShapeDtypeStruct((n, d), F32),
                   jax.ShapeDtypeStruct((d, n), BF16),
                   jax.ShapeDtypeStruct((nk, PEER_NKEYS, n), F32),
                   jax.ShapeDtypeStruct((4 * PEER_HEADS, n), F32),
                   jax.ShapeDtypeStruct((PEER_HEADS * PEER_TOPK, n), F32)],
        compiler_params=_cparams(("parallel",)),
        name="merge",
    )(o_sb, o_sa, sg, x2, ada3, ada3, ada3, g_post, g_pre, wup_sb, wup_sa, wout, wqt, keys)


_ERFC_P = 0.3275911
_ERFC_C = (0.254829592, -0.284496736, 1.421413741, -1.453152027, 1.061405429)


def _gelu(a):
    u = jnp.abs(a)
    t = 1.0 / (1.0 + (_ERFC_P * 2.0 ** -0.5) * u)
    c = [0.5 * v for v in _ERFC_C]
    q = t * (c[0] + t * (c[1] + t * (c[2] + t * (c[3] + t * c[4])))) * jnp.exp2((-0.5 * math.log2(math.e)) * (a * a))
    return a * jnp.where(a >= 0.0, 1.0 - q, q)


BF16_ROWS = 16


def _pair_bits(x):
    u = lax.bitcast_convert_type(x.astype(BF16).astype(F32), jnp.uint32)
    return u | (u >> 16)


def _bcast_bf16(row_u32):
    tm = row_u32.shape[1]
    return pltpu.bitcast(jnp.broadcast_to(row_u32, (BF16_ROWS // 2, tm)), BF16)


def _peer_kernel(h2t_ref, win_ref, woutt_ref, s_ref, st_ref, v2_ref, x1_ref, gate2_ref, gpost_ref, o_ref,
                 acc_scr, act_scr, cnt_scr, e1_scr, rk_scr, e2_scr, *, te):
    k = pl.program_id(1)
    nk = PEER_NKEYS
    tm = h2t_ref.shape[1]

    @pl.when(k == 0)
    def _():
        acc_scr[...] = jnp.zeros(acc_scr.shape, F32)
        for h in range(PEER_HEADS):
            s1 = s_ref[2 * h]
            s2 = s_ref[2 * h + 1]
            tau = st_ref[h:h + 1, :]
            inv_z = st_ref[2 * PEER_HEADS + h:2 * PEER_HEADS + h + 1, :]
            cnt = jnp.zeros(s1.shape, F32)
            rank = jnp.zeros(s2.shape, F32)
            for k2 in range(PEER_TOPK):
                v = v2_ref[h * PEER_TOPK + k2:h * PEER_TOPK + k2 + 1, :]
                cnt = jnp.where(s1 + v >= tau, k2 + 1.0, cnt)
                rank = jnp.where(v > s2, k2 + 1.0, rank)
            cnt_scr[h] = _pair_bits(cnt)
            e1_scr[h] = _pair_bits(jnp.exp(s1 - jnp.max(s1, axis=0, keepdims=True)))
            rk_scr[h] = rank.astype(BF16)
            e2_scr[h] = (jnp.exp(s2 - v2_ref[h * PEER_TOPK:h * PEER_TOPK + 1, :]) * inv_z).astype(BF16)

    a = _dot(win_ref[...], h2t_ref[...])
    gel = _gelu(a)
    grp = nk // BF16_ROWS
    for ii in range(te // nk):
        i = k * (te // nk) + ii
        gate = jnp.zeros((grp, BF16_ROWS, tm), BF16)
        for h in range(PEER_HEADS):
            cnt = _bcast_bf16(cnt_scr[h, pl.ds(i, 1), :])[None]
            e1 = _bcast_bf16(e1_scr[h, pl.ds(i, 1), :])[None]
            rk = rk_scr[h].reshape(grp, BF16_ROWS, tm)
            e2 = e2_scr[h].reshape(grp, BF16_ROWS, tm)
            gate = gate + jnp.where(rk < cnt, e2, jnp.zeros_like(e2)) * e1
        act_scr[ii * nk:(ii + 1) * nk, :] = gate.reshape(nk, tm) * gel[ii * nk:(ii + 1) * nk].astype(BF16)
    acc_scr[...] += _dot(woutt_ref[...], act_scr[...])

    @pl.when(k == pl.num_programs(1) - 1)
    def _():
        o_ref[...] = x1_ref[...] + gate2_ref[...] * _rms(acc_scr[...].T, gpost_ref[...])


def _peer(h2t, w_in_e, w_out_e, s_all, stats, v2top, x1, ada3, g_post, seq, tm, te):
    d, n = h2t.shape
    nt = seq // tm
    n_exp = w_in_e.shape[0]
    win = w_in_e.astype(BF16)
    woutt = w_out_e.astype(BF16).reshape(n_exp // te, te, d).transpose(0, 2, 1)
    once = dict(pipeline_mode=pl.Buffered(1))
    tab_u = pltpu.VMEM((PEER_HEADS, PEER_NKEYS, tm), jnp.uint32)
    tab_b = pltpu.VMEM((PEER_HEADS, PEER_NKEYS, tm), BF16)
    return pl.pallas_call(
        functools.partial(_peer_kernel, te=te),
        grid=(n // tm, n_exp // te),
        in_specs=[
            pl.BlockSpec((d, tm), lambda i, k: (0, i), **once),
            pl.BlockSpec((te, d), lambda i, k: (k, 0)),
            pl.BlockSpec((None, d, te), lambda i, k: (k, 0, 0)),
            pl.BlockSpec((s_all.shape[0], PEER_NKEYS, tm), lambda i, k: (0, 0, i), **once),
            pl.BlockSpec((stats.shape[0], tm), lambda i, k: (0, i), **once),
            pl.BlockSpec((v2top.shape[0], tm), lambda i, k: (0, i), **once),
            pl.BlockSpec((tm, d), lambda i, k: (i, 0), **once),
            pl.BlockSpec((None, 1, d), lambda i, k: (i // nt, 0, 5)),
            pl.BlockSpec((1, d), lambda i, k: (0, 0)),
        ],
        out_specs=pl.BlockSpec((tm, d), lambda i, k: (i, 0)),
        out_shape=jax.ShapeDtypeStruct((n, d), F32),
        scratch_shapes=[pltpu.VMEM((d, tm), F32), pltpu.VMEM((te, tm), BF16), tab_u, tab_u, tab_b, tab_b],
        compiler_params=_cparams(("parallel", "arbitrary")),
        name="peer",
    )(h2t, win, woutt, s_all, stats, v2top, x1, ada3, g_post)


def _layer(x2, c, pos, bsz, seq, w_ada, b_ada, g_pre_mix, g_post_mix, w_in, w_up_sb, w_up_sa, w_out,
           g_pre_ffn, g_post_ffn, w_peer_query, peer_sub_keys, peer_expert_in, peer_expert_out):
    n, d = x2.shape
    tq = 256 if seq % 256 == 0 else 128
    tm = 256
    row1 = lambda g: g.reshape(1, d)
    ada3 = _ada(c, w_ada, b_ada).reshape(bsz, 1, 6 * d)
    (q_sb, kt_sb, v_sb, q_sa, kt_sa, v_sa, q_ix, kt_ix, w_ix, sg) = _inproj(
        x2, pos, ada3, row1(g_pre_mix), w_in, bsz, seq, 512 if seq % 512 == 0 else tm)
    o_sb = _sb_attn(q_sb, kt_sb, v_sb, bsz, seq, 512 if seq % 512 == 0 else tq)
    o_sa = _dsa_attn(q_ix, kt_ix, w_ix, q_sa, kt_sa, v_sa, bsz, seq, tq)
    x1, h2t, s_all, stats, v2top = _merge(o_sb, o_sa, sg, x2, ada3, row1(g_post_mix), row1(g_pre_ffn),
                                          w_up_sb, w_up_sa, w_out, w_peer_query, peer_sub_keys, seq, tm)
    return _peer(h2t, peer_expert_in, peer_expert_out, s_all, stats, v2top, x1, ada3, row1(g_post_ffn),
                 seq, tm=min(1024, seq), te=512)


def kernel(x, c, positions, w_ada, b_ada, g_pre_mix, g_post_mix, w_in, w_up_sb, w_up_sa, w_out,
           g_pre_ffn, g_post_ffn, w_peer_query, peer_sub_keys, peer_expert_in, peer_expert_out):
    bsz, seq, d = x.shape
    x2 = x.reshape(bsz * seq, d)
    for l in range(w_ada.shape[0]):
        x2 = _layer(x2, c, positions, bsz, seq, w_ada[l], b_ada[l], g_pre_mix[l], g_post_mix[l],
                    w_in[l], w_up_sb[l], w_up_sa[l], w_out[l], g_pre_ffn[l], g_post_ffn[l],
                    w_peer_query[l], peer_sub_keys[l], peer_expert_in[l], peer_expert_out[l])
    return x2.reshape(bsz, seq, d)
```

```python
import functools
import math

import numpy as np
import jax
import jax.numpy as jnp
from jax import lax
from jax.experimental import pallas as pl
from jax.experimental.pallas import tpu as pltpu

HEAD_DIM = 64
N_HEADS_SB = 8
N_HEADS_SA = 8
N_KV_SA = 2
N_IDX_HEADS = 8
IDX_DIM = 64
TOPK_MAX = 256
ROPE_THETA = 500000.0
ROT_DIM = HEAD_DIM // 4
ROT_HALF = ROT_DIM // 2
PEER_HEADS = 8
PEER_NKEYS = 128
PEER_QDIM = 256
PEER_TOPK = 16
NORM_EPS = 1e-6

LANES = 128
NEG_BIG = -1e30
VMEM_LIMIT = 56 * 1024 * 1024

F32 = jnp.float32
BF16 = jnp.bfloat16


def _cparams(sem):
    return pltpu.CompilerParams(dimension_semantics=sem, vmem_limit_bytes=VMEM_LIMIT)


def _dot(a, b):
    return jnp.dot(a, b, preferred_element_type=F32)


def _dot_nt(a, b):
    return lax.dot_general(a, b, (((1,), (1,)), ((), ())), preferred_element_type=F32)


def _split(a):
    hi = a.astype(BF16)
    lo = (a - hi.astype(F32)).astype(BF16)
    return hi, lo


def _dot3(a, b):
    ah, al = _split(a)
    bh, bl = _split(b)
    return _dot(ah, bh) + (_dot(ah, bl) + _dot(al, bh))


def _rms(x, g):
    return x * lax.rsqrt(jnp.mean(x * x, axis=-1, keepdims=True) + NORM_EPS) * g


def _ada_kernel(c_ref, w_ref, b_ref, o_ref):
    c = c_ref[...]
    s = c / (1.0 + jnp.exp(-c))
    o_ref[...] = _dot3(s, w_ref[...]) + b_ref[...]


def _ada(c, w, b):
    bsz, d = c.shape
    n_out = w.shape[1]
    return pl.pallas_call(
        _ada_kernel,
        grid=(n_out // d,),
        in_specs=[
            pl.BlockSpec((bsz, d), lambda j: (0, 0)),
            pl.BlockSpec((d, d), lambda j: (0, j)),
            pl.BlockSpec((1, d), lambda j: (0, j)),
        ],
        out_specs=pl.BlockSpec((bsz, d), lambda j: (0, j)),
        out_shape=jax.ShapeDtypeStruct((bsz, n_out), F32),
        compiler_params=_cparams(("parallel",)),
        name="ada",
    )(c, w, b.reshape(1, n_out))


def _rope_rows(x, cos_t, sin_t):
    lane = lax.broadcasted_iota(jnp.int32, (1, LANES), 1) % HEAD_DIM
    c = jnp.where(lane < ROT_DIM, cos_t, 1.0)
    s_lo = jnp.where(lane < ROT_HALF, -sin_t, 0.0)
    s_hi = jnp.where((lane >= ROT_HALF) & (lane < ROT_DIM), sin_t, 0.0)
    outs = []
    for j in range(x.shape[1] // LANES):
        xb = x[:, j * LANES:(j + 1) * LANES]
        up = pltpu.roll(xb, LANES - ROT_HALF, 1)
        dn = pltpu.roll(xb, ROT_HALF, 1)
        outs.append(xb * c + up * s_lo + dn * s_hi)
    return jnp.concatenate(outs, axis=1) if len(outs) > 1 else outs[0]


def _rope_cols(x, cos_t, sin_t):
    outs = []
    for j in range(x.shape[0] // HEAD_DIM):
        blk = x[j * HEAD_DIM:(j + 1) * HEAD_DIM]
        x1 = blk[0:ROT_HALF]
        x2 = blk[ROT_HALF:ROT_DIM]
        outs += [x1 * cos_t - x2 * sin_t, x2 * cos_t + x1 * sin_t, blk[ROT_DIM:]]
    return jnp.concatenate(outs, axis=0)


def _inproj_kernel(x_ref, posc_ref, posr_ref, shift_ref, scale_ref, g_ref, invl_ref, invc_ref,
                   wq_sb, wkt_sb, wv_sb, wq_sa, wkt_sa, wv_sa, wq_ix, wkt_ix, w_aux, w_gate,
                   q_sb_o, kt_sb_o, v_sb_o, q_sa_o, kt_sa_o, v_sa_o, q_ix_o, kt_ix_o, wix_o, sg_o):
    x = x_ref[...]
    h = _rms(x, g_ref[...]) * (1.0 + scale_ref[...]) + shift_ref[...]
    hb = h.astype(BF16)

    ang_r = posc_ref[...].astype(F32) * invl_ref[...]
    cos_r, sin_r = jnp.cos(ang_r), jnp.sin(ang_r)
    ang_c = posr_ref[...].astype(F32) * invc_ref[...]
    cos_c, sin_c = jnp.cos(ang_c), jnp.sin(ang_c)

    scale = HEAD_DIM ** -0.5
    q_sb_o[...] = (_dot(hb, wq_sb[...]) * scale).astype(BF16)
    kt_sb_o[...] = _dot_nt(wkt_sb[...], hb).astype(BF16)
    v_sb_o[...] = _dot(hb, wv_sb[...]).astype(BF16)
    q_sa_o[...] = (_rope_rows(_dot(hb, wq_sa[...]), cos_r, sin_r) * scale).astype(BF16)
    kt_sa_o[...] = _rope_cols(_dot_nt(wkt_sa[...], hb), cos_c, sin_c).astype(BF16)
    v_sa_o[...] = _dot(hb, wv_sa[...]).astype(BF16)
    q_ix_o[...] = _rope_rows(_dot(hb, wq_ix[...]), cos_r, sin_r).astype(BF16)
    kt_ix_o[...] = _rope_cols(_dot_nt(wkt_ix[...], hb), cos_c, sin_c).astype(BF16)
    wix_o[...] = _dot(hb, w_aux[...]) * (IDX_DIM ** -0.5 * N_IDX_HEADS ** -0.5)
    gate = _dot(hb, w_gate[...])
    sg_o[...] = (1.0 / (1.0 + jnp.exp(-gate))).astype(BF16)


def _inproj(x2, pos, shift_scale, g_pre, w_in, bsz, seq, tm):
    n, d = x2.shape
    w_sb = N_HEADS_SB * HEAD_DIM
    w_sa = N_HEADS_SA * HEAD_DIM
    w_kv = N_KV_SA * HEAD_DIM
    w_ixq = N_IDX_HEADS * IDX_DIM
    cuts = np.cumsum([w_sb, w_sb, w_sb, w_sa, w_kv, w_kv, w_ixq, IDX_DIM, N_IDX_HEADS, d, d])
    c = [0] + [int(v) for v in cuts]
    wb = w_in.astype(BF16)
    seg = lambda i: wb[:, c[i]:c[i + 1]]
    wq_sb, wk_sb, wv_sb, wq_sa, wk_sa, wv_sa, wq_ix, wk_ix, w_ixw = [seg(i) for i in range(9)]
    w_gate = wb[:, c[9]:c[11]]
    dup = lambda w: jnp.concatenate(
        [w[:, j * HEAD_DIM:(j + 1) * HEAD_DIM] for j in range(w.shape[1] // HEAD_DIM) for _ in (0, 1)], axis=1)
    wkt_sb = wk_sb.T
    wkt_sa = dup(wk_sa).T
    wv_sa2 = dup(wv_sa)
    wkt_ix = dup(wk_ix).T
    w_aux = jnp.concatenate([w_ixw, jnp.zeros((d, LANES - N_IDX_HEADS), BF16)], axis=1)

    inv_freq = jnp.power(ROPE_THETA, -2.0 * jnp.arange(ROT_HALF, dtype=F32) / ROT_DIM)
    inv_lane = jnp.tile(inv_freq, LANES // ROT_HALF).reshape(1, LANES)
    inv_col = jnp.broadcast_to(inv_freq[:, None], (ROT_HALF, tm))

    nt = seq // tm
    row = lambda w: pl.BlockSpec((tm, w), lambda i: (i, 0))
    colT = lambda r: pl.BlockSpec((None, r, tm), lambda i: (i // nt, 0, i % nt))
    const = lambda a: pl.BlockSpec(a.shape, lambda i: (0,) * a.ndim)
    weights = [wq_sb, wkt_sb, wv_sb, wq_sa, wkt_sa, wv_sa2, wq_ix, wkt_ix, w_aux, w_gate]
    out_shape = [
        jax.ShapeDtypeStruct((n, w_sb), BF16),
        jax.ShapeDtypeStruct((bsz, w_sb, seq), BF16),
        jax.ShapeDtypeStruct((n, w_sb), BF16),
        jax.ShapeDtypeStruct((n, w_sa), BF16),
        jax.ShapeDtypeStruct((bsz, 2 * w_kv, seq), BF16),
        jax.ShapeDtypeStruct((n, 2 * w_kv), BF16),
        jax.ShapeDtypeStruct((n, w_ixq), BF16),
        jax.ShapeDtypeStruct((bsz, 2 * IDX_DIM, seq), BF16),
        jax.ShapeDtypeStruct((n, LANES), F32),
        jax.ShapeDtypeStruct((n, 2 * d), BF16),
    ]
    out_specs = [row(w_sb), colT(w_sb), row(w_sb), row(w_sa), colT(2 * w_kv), row(2 * w_kv),
                 row(w_ixq), colT(2 * IDX_DIM), row(LANES), row(2 * d)]
    return pl.pallas_call(
        _inproj_kernel,
        grid=(n // tm,),
        in_specs=[
            row(d),
            pl.BlockSpec((tm, 1), lambda i: (i, 0)),
            pl.BlockSpec((None, 1, tm), lambda i: (i // nt, 0, i % nt)),
            pl.BlockSpec((None, 1, d), lambda i: (i // nt, 0, 0)),
            pl.BlockSpec((None, 1, d), lambda i: (i // nt, 0, 1)),
            const(g_pre), const(inv_lane), const(inv_col),
        ] + [const(w) for w in weights],
        out_specs=out_specs,
        out_shape=out_shape,
        compiler_params=_cparams(("parallel",)),
        name="inproj",
    )(x2, pos.reshape(n, 1), pos.reshape(bsz, 1, seq), shift_scale, shift_scale, g_pre,
      inv_lane, inv_col, *weights)


def _softplus(z):
    return jnp.maximum(z, 0.0) + jnp.log(1.0 + jnp.exp(-jnp.abs(z)))


def _sb_kernel(q_ref, kt_ref, v_ref, o_ref, *, tq):
    qi = pl.program_id(2)
    q2 = q_ref[...]
    lane_half = lax.broadcasted_iota(jnp.int32, (1, LANES), 1) // HEAD_DIM
    r_i = lax.broadcasted_iota(jnp.int32, (tq, tq), 0)
    c_i = lax.broadcasted_iota(jnp.int32, (tq, tq), 1)
    strict = c_i < r_i
    ck = min(tq, 2 * LANES)
    later = jnp.where(lax.broadcasted_iota(jnp.int32, (ck, ck), 0) > lax.broadcasted_iota(jnp.int32, (ck, ck), 1),
                      1.0, 0.0).astype(BF16)
    qms = [jnp.where(lane_half == hh, q2, jnp.zeros_like(q2)) for hh in (0, 1)]

    def block(kb, state, diag):
        k0 = pl.multiple_of(kb * tq, tq)
        kt = kt_ref[:, pl.ds(k0, tq)]
        v = v_ref[pl.ds(k0, tq), :]
        new = []
        for hh in (0, 1):
            carry, acc = state[2 * hh], state[2 * hh + 1]
            z = _dot(qms[hh], kt)
            sp = _softplus(z)
            lf = -sp
            if diag:
                lf = jnp.where(strict, lf, 0.0)
            chunks = [None] * (tq // ck)
            for c in reversed(range(tq // ck)):
                lf_c = lf[:, c * ck:(c + 1) * ck]
                chunks[c] = _dot(lf_c.astype(BF16), later) + carry
                carry = carry + jnp.sum(lf_c, axis=1, keepdims=True)
            between = jnp.concatenate(chunks, axis=1) if len(chunks) > 1 else chunks[0]
            a = jnp.exp((z - sp) + between)
            if diag:
                a = jnp.where(strict, a, 0.0)
            new += [carry, acc + _dot(a.astype(BF16), v)]
        return tuple(new)

    zero = (jnp.zeros((tq, 1), F32), jnp.zeros((tq, LANES), F32))
    state = block(qi, zero + zero, True)
    state = lax.fori_loop(0, qi, lambda it, st: block(qi - 1 - it, st, False), state)
    o_ref[...] = jnp.where(lane_half == 0, state[1], state[3]).astype(o_ref.dtype)


def _sb_attn(q, kt, v, bsz, seq, tq):
    n, w = q.shape
    npair = w // LANES
    nq = seq // tq
    return pl.pallas_call(
        functools.partial(_sb_kernel, tq=tq),
        grid=(bsz, npair, nq),
        in_specs=[
            pl.BlockSpec((tq, LANES), lambda b, p, i: (b * nq + i, p)),
            pl.BlockSpec((None, LANES, seq), lambda b, p, i: (b, p, 0)),
            pl.BlockSpec((seq, LANES), lambda b, p, i: (b, p)),
        ],
        out_specs=pl.BlockSpec((tq, LANES), lambda b, p, i: (b * nq + i, p)),
        out_shape=jax.ShapeDtypeStruct((n, w), BF16),
        compiler_params=_cparams(("parallel", "parallel", "arbitrary")),
        name="sb_attn",
    )(q, kt, v)


def _sort_key(s):
    bits = lax.bitcast_convert_type(s, jnp.int32)
    bits = jnp.where(bits == jnp.int32(-2 ** 31), 0, bits)
    return bits ^ ((bits >> 31) & jnp.int32(0x7FFFFFFF))


def _dsa_kernel(qix_ref, ktix_ref, wix_ref, q_ref, kt_ref, v_ref, o_ref,
                qm_scr, key_scr, bias_scr, qs_scr, m_scr, acc_scr, tau_scr, ngt_scr, *, tq, nq, k_sel):
    qi = pl.program_id(1)
    nkb = qi + 1
    lane_half = lax.broadcasted_iota(jnp.int32, (1, LANES), 1) // HEAD_DIM
    r_i = lax.broadcasted_iota(jnp.int32, (tq, tq), 0)
    c_i = lax.broadcasted_iota(jnp.int32, (tq, tq), 1)
    upto = jnp.where(r_i <= c_i, 1.0, 0.0).astype(BF16)

    for h in range(N_IDX_HEADS):
        qp = qix_ref[:, (h // 2) * LANES:(h // 2 + 1) * LANES]
        qm_scr[h] = jnp.where(lane_half == h % 2, qp, jnp.zeros_like(qp))

    def score_body(kb, _):
        k0 = pl.multiple_of(kb * tq, tq)
        kk = ktix_ref[:, pl.ds(k0, tq)]
        sc = jnp.zeros((tq, tq), F32)
        for h in range(N_IDX_HEADS):
            rel = jnp.maximum(_dot(qm_scr[h], kk), 0.0)
            sc = sc + rel * wix_ref[:, h:h + 1]
        causal = (c_i + k0) <= (r_i + qi * tq)
        sc = jnp.where(causal, sc, -jnp.inf)
        key_scr[:, pl.ds(k0, tq)] = _sort_key(sc)
        return 0

    lax.fori_loop(0, nkb, score_body, 0)

    n_grp = 4
    rg = tq // n_grp

    def search(nblk):
        def count_ge(g, cand):
            cand_b = jnp.broadcast_to(cand, (rg, LANES))
            part = jnp.zeros((rg, LANES), jnp.int32)
            for j in range(nblk * tq // LANES):
                blk = key_scr[g * rg:(g + 1) * rg, j * LANES:(j + 1) * LANES]
                part = part + jnp.where(blk >= cand_b, 1, 0)
            return jnp.sum(part, axis=1, keepdims=True)

        def bit_body(it, curs):
            bit = jnp.int32(1) << (31 - it)
            return tuple(jnp.where(count_ge(g, curs[g] + bit) >= k_sel, curs[g] + bit, curs[g])
                         for g in range(n_grp))

        lowest = jnp.full((rg, 1), -2 ** 31, jnp.int32)
        curs = lax.fori_loop(0, 32, bit_body, (lowest,) * n_grp)
        for g in range(n_grp):
            tau_scr[g * rg:(g + 1) * rg, :] = curs[g]
            ngt_scr[g * rg:(g + 1) * rg, :] = count_ge(g, curs[g] + 1)

    for q in range(nq):
        pl.when(qi == q)(functools.partial(search, q + 1))

    tau = tau_scr[...]
    need = (k_sel - ngt_scr[...]).astype(F32)

    def bias_body(kb, seen):
        k0 = pl.multiple_of(kb * tq, tq)
        key = key_scr[:, pl.ds(k0, tq)]
        tie = key == tau
        rank = _dot(jnp.where(tie, 1.0, 0.0).astype(BF16), upto) + seen
        sel = (key > tau) | (tie & (rank <= need))
        causal = (c_i + k0) <= (r_i + qi * tq)
        bias_scr[:, pl.ds(k0, tq)] = jnp.where(sel & causal, 0.0, NEG_BIG)
        return rank[:, tq - 1:tq]

    lax.fori_loop(0, nkb, bias_body, jnp.zeros((tq, 1), F32))

    gsz = N_HEADS_SA // N_KV_SA
    for g in range(N_KV_SA):
        for hh in range(gsz):
            h = g * gsz + hh
            qp = q_ref[:, (h // 2) * LANES:(h // 2 + 1) * LANES]
            qs_scr[hh * tq:(hh + 1) * tq, :] = jnp.where(lane_half == h % 2, qp, jnp.zeros_like(qp))
        m_scr[...] = jnp.full(m_scr.shape, -jnp.inf, F32)
        acc_scr[...] = jnp.zeros(acc_scr.shape, F32)

        def masked_logits(k0, w, g=g):
            kk = kt_ref[g * LANES:(g + 1) * LANES, pl.ds(k0, w)]
            logits = _dot(qs_scr[...], kk)
            return (logits.reshape(gsz, tq, w) + bias_scr[:, pl.ds(k0, w)][None]).reshape(gsz * tq, w)

        def max_step(k0, w):
            logits = masked_logits(k0, w)
            part = m_scr[...]
            for j in range(w // LANES):
                part = jnp.maximum(part, logits[:, j * LANES:(j + 1) * LANES])
            m_scr[...] = part

        def attn_step(k0, w, g=g):
            logits = masked_logits(k0, w)
            mrow = m_scr[...]
            p = jnp.concatenate([jnp.exp(logits[:, j * LANES:(j + 1) * LANES] - mrow)
                                 for j in range(w // LANES)], axis=1).astype(BF16)
            vv = jnp.concatenate([v_ref[pl.ds(k0, w), g * LANES:(g + 1) * LANES],
                                  jnp.ones((w, LANES), BF16)], axis=1)
            acc_scr[...] += _dot(p, vv)

        def sweep(step):
            def pair_body(it, _):
                step(pl.multiple_of(it * (2 * tq), 2 * tq), 2 * tq)
                return 0
            lax.fori_loop(0, nkb // 2, pair_body, 0)
            pl.when(nkb % 2 == 1)(lambda: step(pl.multiple_of((nkb - 1) * tq, tq), tq))

        sweep(max_step)
        m_scr[...] = jnp.broadcast_to(jnp.max(m_scr[...], axis=1, keepdims=True), m_scr.shape)
        sweep(attn_step)
        for hp in range(gsz // 2):
            pair = []
            for hh in (2 * hp, 2 * hp + 1):
                a = acc_scr[hh * tq:(hh + 1) * tq, :]
                pair.append(a[:, :LANES] / a[:, LANES:LANES + 1])
            col = (g * gsz // 2 + hp) * LANES
            o_ref[:, col:col + LANES] = jnp.where(lane_half == 0, pair[0], pair[1]).astype(o_ref.dtype)


def _dsa_attn(q_ix, kt_ix, w_ix, q, kt, v, bsz, seq, tq):
    n, w = q.shape
    nq = seq // tq
    k_sel = min(TOPK_MAX, seq // 4)
    gsz = N_HEADS_SA // N_KV_SA
    rowb = lambda width: pl.BlockSpec((tq, width), lambda b, i: (b * nq + i, 0))
    return pl.pallas_call(
        functools.partial(_dsa_kernel, tq=tq, nq=nq, k_sel=k_sel),
        grid=(bsz, nq),
        in_specs=[
            rowb(q_ix.shape[1]),
            pl.BlockSpec((None, kt_ix.shape[1], seq), lambda b, i: (b, 0, 0)),
            rowb(LANES),
            rowb(w),
            pl.BlockSpec((None, kt.shape[1], seq), lambda b, i: (b, 0, 0)),
            pl.BlockSpec((seq, v.shape[1]), lambda b, i: (b, 0)),
        ],
        out_specs=rowb(w),
        out_shape=jax.ShapeDtypeStruct((n, w), BF16),
        scratch_shapes=[
            pltpu.VMEM((N_IDX_HEADS, tq, LANES), BF16),
            pltpu.VMEM((tq, seq), jnp.int32),
            pltpu.VMEM((tq, seq), F32),
            pltpu.VMEM((gsz * tq, LANES), BF16),
            pltpu.VMEM((gsz * tq, LANES), F32),
            pltpu.VMEM((gsz * tq, 2 * LANES), F32),
            pltpu.VMEM((tq, 1), jnp.int32),
            pltpu.VMEM((tq, 1), jnp.int32),
        ],
        compiler_params=_cparams(("parallel", "arbitrary")),
        name="dsa_attn",
    )(q_ix, kt_ix, w_ix, q, kt, v)


def _put_row(stack, r, row):
    idx = lax.broadcasted_iota(jnp.int32, stack.shape, 0)
    return jnp.where(idx == r, row, stack)


SUBLANES = 8


def _oddeven_pairs(n):
    pairs, p = [], 1
    while p < n:
        k = p
        while k >= 1:
            for j in range(k % p, n - k, 2 * k):
                for i in range(min(k, n - j - k)):
                    if (i + j) // (2 * p) == (i + j + k) // (2 * p):
                        pairs.append((i + j, i + j + k))
            k //= 2
        p *= 2
    return pairs


def _exchange(a, b):
    if a is None:
        return b, None
    if b is None:
        return a, None
    return jnp.maximum(a, b), jnp.minimum(a, b)


def _top_desc(x, k):
    rows, tm = x.shape
    v = [x[SUBLANES * r:SUBLANES * (r + 1)] for r in range(rows // SUBLANES)] + [None] * (k - rows // SUBLANES)
    for i, j in _oddeven_pairs(k):
        v[i], v[j] = _exchange(v[i], v[j])
    shift = SUBLANES // 2
    while shift >= 1:
        rolled = [None if a is None else pltpu.roll(a, shift, 0) for a in v]
        merged = []
        for r in range(k):
            a, b = v[r], rolled[k - 1 - r]
            merged.append(b if a is None else (a if b is None else jnp.maximum(a, b)))
        v = merged
        d = k // 2
        while d >= 1:
            for r in range(k):
                if (r // d) % 2 == 0:
                    v[r], v[r + d] = _exchange(v[r], v[r + d])
            d //= 2
        shift //= 2
    sub = lax.broadcasted_iota(jnp.int32, (SUBLANES, tm), 0)
    slabs = []
    for s0 in range(0, k, SUBLANES):
        slab = v[s0]
        for r in range(1, SUBLANES):
            slab = jnp.where(sub == r, v[s0 + r], slab)
        slabs.append(slab)
    return jnp.concatenate(slabs, axis=0)


def _merge_kernel(osb_ref, osa_ref, sg_ref, x_ref, gate1_ref, shift2_ref, scale2_ref,
                  gpost_ref, gpre_ref, wup_sb, wup_sa, wout, wqt, keys_ref,
                  x1_o, h2t_o, s_o, st_o, v2_o):
    d = x_ref.shape[1]
    y_sb = _dot(osb_ref[...], wup_sb[...])
    y_sa = _dot(osa_ref[...], wup_sa[...])
    sg = sg_ref[...].astype(F32)
    merged = sg[:, :d] * y_sb + sg[:, d:] * y_sa
    y = _dot(merged.astype(BF16), wout[...])
    x1 = x_ref[...] + gate1_ref[...] * _rms(y, gpost_ref[...])
    x1_o[...] = x1
    h2 = _rms(x1, gpre_ref[...]) * (1.0 + scale2_ref[...]) + shift2_ref[...]
    h2t = h2.T.astype(BF16)
    h2t_o[...] = h2t
    qt = _dot(wqt[...], h2t)

    half = PEER_QDIM // 2
    stats = jnp.zeros(st_o.shape, F32)
    for h in range(PEER_HEADS):
        tops = []
        for p in (0, 1):
            r0 = (2 * h + p) * half
            q_hi, q_lo = _split(qt[r0:r0 + half])
            kf = keys_ref[2 * h + p]
            k_hi, k_lo = _split(kf)
            s = _dot(k_hi, q_hi) + (_dot(k_hi, q_lo) + _dot(k_lo, q_hi))
            s_o[2 * h + p] = s
            tops.append(_top_desc(s, PEER_TOPK))
        v2_o[h * PEER_TOPK:(h + 1) * PEER_TOPK, :] = tops[1]
        hk = PEER_TOPK // 2
        cand = jnp.concatenate(
            [tops[0][0:1] + tops[1]]
            + [tops[0][k1:k1 + 1] + tops[1][0:hk] for k1 in range(1, hk)]
            + [tops[0][hk:] + tops[1][0:1]], axis=0)
        best = _top_desc(cand, PEER_TOPK)
        mx = best[0:1]
        z = jnp.sum(jnp.exp(best - mx), axis=0, keepdims=True)
        stats = _put_row(stats, h, best[PEER_TOPK - 1:PEER_TOPK])
        stats = _put_row(stats, PEER_HEADS + h, mx)
        stats = _put_row(stats, 2 * PEER_HEADS + h, 1.0 / z)
    st_o[...] = stats


def _merge(o_sb, o_sa, sg, x2, ada3, g_post, g_pre, w_up_sb, w_up_sa, w_out, w_query, sub_keys,
           seq, tm):
    n, d = x2.shape
    nt = seq // tm
    nk = 2 * PEER_HEADS
    wqt = w_query.T.astype(BF16)
    keys = sub_keys.reshape(nk, PEER_NKEYS, PEER_QDIM // 2)
    wup_sb, wup_sa, wout = w_up_sb.astype(BF16), w_up_sa.astype(BF16), w_out.astype(BF16)
    row = lambda w: pl.BlockSpec((tm, w), lambda i: (i, 0))
    const = lambda a: pl.BlockSpec(a.shape, lambda i: (0,) * a.ndim)
    adab = lambda j: pl.BlockSpec((None, 1, d), lambda i: (i // nt, 0, j))
    return pl.pallas_call(
        _merge_kernel,
        grid=(n // tm,),
        in_specs=[row(o_sb.shape[1]), row(o_sa.shape[1]), row(2 * d), row(d),
                  adab(2), adab(3), adab(4), const(g_post), const(g_pre),
                  const(wup_sb), const(wup_sa), const(wout), const(wqt), const(keys)],
        out_specs=[row(d),
                   pl.BlockSpec((d, tm), lambda i: (0, i)),
                   pl.BlockSpec((nk, PEER_NKEYS, tm), lambda i: (0, 0, i)),
                   pl.BlockSpec((4 * PEER_HEADS, tm), lambda i: (0, i)),
                   pl.BlockSpec((PEER_HEADS * PEER_TOPK, tm), lambda i: (0, i))],
        out_shape=[jax.ShapeDtypeStruct((n, d), F32),
                   jax.ShapeDtypeStruct((d, n), BF16),
                   jax.ShapeDtypeStruct((nk, PEER_NKEYS, n), F32),
                   jax.ShapeDtypeStruct((4 * PEER_HEADS, n), F32),
                   jax.ShapeDtypeStruct((PEER_HEADS * PEER_TOPK, n), F32)],
        compiler_params=_cparams(("parallel",)),
        name="merge",
    )(o_sb, o_sa, sg, x2, ada3, ada3, ada3, g_post, g_pre, wup_sb, wup_sa, wout, wqt, keys)


_ERFC_P = 0.3275911
_ERFC_C = (0.254829592, -0.284496736, 1.421413741, -1.453152027, 1.061405429)


def _gelu(a):
    u = jnp.abs(a)
    t = 1.0 / (1.0 + (_ERFC_P * 2.0 ** -0.5) * u)
    c = [0.5 * v for v in _ERFC_C]
    q = t * (c[0] + t * (c[1] + t * (c[2] + t * (c[3] + t * c[4])))) * jnp.exp2((-0.5 * math.log2(math.e)) * (a * a))
    return a * jnp.where(a >= 0.0, 1.0 - q, q)


BF16_ROWS = 16


def _pair_bits(x):
    u = lax.bitcast_convert_type(x.astype(BF16).astype(F32), jnp.uint32)
    return u | (u >> 16)


def _bcast_bf16(row_u32):
    tm = row_u32.shape[1]
    return pltpu.bitcast(jnp.broadcast_to(row_u32, (BF16_ROWS // 2, tm)), BF16)


def _peer_kernel(h2t_ref, win_ref, woutt_ref, s_ref, st_ref, v2_ref, x1_ref, gate2_ref, gpost_ref, o_ref,
                 acc_scr, act_scr, cnt_scr, e1_scr, rk_scr, e2_scr, *, te):
    k = pl.program_id(1)
    nk = PEER_NKEYS
    tm = h2t_ref.shape[1]

    @pl.when(k == 0)
    def _():
        acc_scr[...] = jnp.zeros(acc_scr.shape, F32)
        for h in range(PEER_HEADS):
            s1 = s_ref[2 * h]
            s2 = s_ref[2 * h + 1]
            tau = st_ref[h:h + 1, :]
            inv_z = st_ref[2 * PEER_HEADS + h:2 * PEER_HEADS + h + 1, :]
            top1 = jnp.max(s1, axis=0, keepdims=True)
            cnt = jnp.zeros(s1.shape, F32)
            rank = jnp.zeros(s2.shape, F32)
            tail = jnp.zeros(top1.shape, F32)
            half = PEER_TOPK // 2
            for k2 in range(PEER_TOPK):
                v = v2_ref[h * PEER_TOPK + k2:h * PEER_TOPK + k2 + 1, :]
                if k2 < half:
                    cnt = jnp.where(s1 + v >= tau, k2 + 1.0, cnt)
                else:
                    tail = jnp.where(top1 + v >= tau, k2 + 1.0, tail)
                rank = jnp.where(v > s2, k2 + 1.0, rank)
            cnt = jnp.where(s1 == top1, jnp.maximum(cnt, tail), cnt)
            cnt_scr[h] = _pair_bits(cnt)
            e1_scr[h] = _pair_bits(jnp.exp(s1 - top1))
            rk_scr[h] = rank.astype(BF16)
            e2_scr[h] = (jnp.exp(s2 - v2_ref[h * PEER_TOPK:h * PEER_TOPK + 1, :]) * inv_z).astype(BF16)

    a = _dot(win_ref[...], h2t_ref[...])
    gel = _gelu(a)
    grp = nk // BF16_ROWS
    for ii in range(te // nk):
        i = k * (te // nk) + ii
        gate = jnp.zeros((grp, BF16_ROWS, tm), BF16)
        for h in range(PEER_HEADS):
            cnt = _bcast_bf16(cnt_scr[h, pl.ds(i, 1), :])[None]
            e1 = _bcast_bf16(e1_scr[h, pl.ds(i, 1), :])[None]
            rk = rk_scr[h].reshape(grp, BF16_ROWS, tm)
            e2 = e2_scr[h].reshape(grp, BF16_ROWS, tm)
            gate = gate + jnp.where(rk < cnt, e2, jnp.zeros_like(e2)) * e1
        act_scr[ii * nk:(ii + 1) * nk, :] = gate.reshape(nk, tm) * gel[ii * nk:(ii + 1) * nk].astype(BF16)
    acc_scr[...] += _dot(woutt_ref[...], act_scr[...])

    @pl.when(k == pl.num_programs(1) - 1)
    def _():
        o_ref[...] = x1_ref[...] + gate2_ref[...] * _rms(acc_scr[...].T, gpost_ref[...])


def _peer(h2t, w_in_e, w_out_e, s_all, stats, v2top, x1, ada3, g_post, seq, tm, te):
    d, n = h2t.shape
    nt = seq // tm
    n_exp = w_in_e.shape[0]
    win = w_in_e.astype(BF16)
    woutt = w_out_e.astype(BF16).reshape(n_exp // te, te, d).transpose(0, 2, 1)
    once = dict(pipeline_mode=pl.Buffered(1))
    tab_u = pltpu.VMEM((PEER_HEADS, PEER_NKEYS, tm), jnp.uint32)
    tab_b = pltpu.VMEM((PEER_HEADS, PEER_NKEYS, tm), BF16)
    return pl.pallas_call(
        functools.partial(_peer_kernel, te=te),
        grid=(n // tm, n_exp // te),
        in_specs=[
            pl.BlockSpec((d, tm), lambda i, k: (0, i), **once),
            pl.BlockSpec((te, d), lambda i, k: (k, 0)),
            pl.BlockSpec((None, d, te), lambda i, k: (k, 0, 0)),
            pl.BlockSpec((s_all.shape[0], PEER_NKEYS, tm), lambda i, k: (0, 0, i), **once),
            pl.BlockSpec((stats.shape[0], tm), lambda i, k: (0, i), **once),
            pl.BlockSpec((v2top.shape[0], tm), lambda i, k: (0, i), **once),
            pl.BlockSpec((tm, d), lambda i, k: (i, 0), **once),
            pl.BlockSpec((None, 1, d), lambda i, k: (i // nt, 0, 5)),
            pl.BlockSpec((1, d), lambda i, k: (0, 0)),
        ],
        out_specs=pl.BlockSpec((tm, d), lambda i, k: (i, 0)),
        out_shape=jax.ShapeDtypeStruct((n, d), F32),
        scratch_shapes=[pltpu.VMEM((d, tm), F32), pltpu.VMEM((te, tm), BF16), tab_u, tab_u, tab_b, tab_b],
        compiler_params=_cparams(("parallel", "arbitrary")),
        name="peer",
    )(h2t, win, woutt, s_all, stats, v2top, x1, ada3, g_post)


def _layer(x2, c, pos, bsz, seq, w_ada, b_ada, g_pre_mix, g_post_mix, w_in, w_up_sb, w_up_sa, w_out,
           g_pre_ffn, g_post_ffn, w_peer_query, peer_sub_keys, peer_expert_in, peer_expert_out):
    n, d = x2.shape
    tq = 256 if seq % 256 == 0 else 128
    tm = 256
    row1 = lambda g: g.reshape(1, d)
    ada3 = _ada(c, w_ada, b_ada).reshape(bsz, 1, 6 * d)
    (q_sb, kt_sb, v_sb, q_sa, kt_sa, v_sa, q_ix, kt_ix, w_ix, sg) = _inproj(
        x2, pos, ada3, row1(g_pre_mix), w_in, bsz, seq, 512 if seq % 512 == 0 else tm)
    o_sb = _sb_attn(q_sb, kt_sb, v_sb, bsz, seq, 512 if seq % 512 == 0 else tq)
    o_sa = _dsa_attn(q_ix, kt_ix, w_ix, q_sa, kt_sa, v_sa, bsz, seq, tq)
    x1, h2t, s_all, stats, v2top = _merge(o_sb, o_sa, sg, x2, ada3, row1(g_post_mix), row1(g_pre_ffn),
                                          w_up_sb, w_up_sa, w_out, w_peer_query, peer_sub_keys, seq, tm)
    return _peer(h2t, peer_expert_in, peer_expert_out, s_all, stats, v2top, x1, ada3, row1(g_post_ffn),
                 seq, tm=min(1024, seq), te=512)


def kernel(x, c, positions, w_ada, b_ada, g_pre_mix, g_post_mix, w_in, w_up_sb, w_up_sa, w_out,
           g_pre_ffn, g_post_ffn, w_peer_query, peer_sub_keys, peer_expert_in, peer_expert_out):
    bsz, seq, d = x.shape
    x2 = x.reshape(bsz * seq, d)
    for l in range(w_ada.shape[0]):
        x2 = _layer(x2, c, positions, bsz, seq, w_ada[l], b_ada[l], g_pre_mix[l], g_post_mix[l],
                    w_in[l], w_up_sb[l], w_up_sa[l], w_out[l], g_pre_ffn[l], g_post_ffn[l],
                    w_peer_query[l], peer_sub_keys[l], peer_expert_in[l], peer_expert_out[l])
    return x2.reshape(bsz, seq, d)
```
